```python
import math, functools
import jax, jax.numpy as jnp
from jax import lax
import numpy as np

D_MODEL = 1024
BATCH = 2
SEQ = 8192
DEPTH = 1
DEC_BATCH = 128
DEC_SEQ = 1
PAST_LEN = 8192
PAGE_SIZE = 128

N_SSM_HEADS = 16
SSM_HEAD_DIM = 64
D_SSM = N_SSM_HEADS * SSM_HEAD_DIM
N_GROUPS = 2
HEADS_PER_GROUP = N_SSM_HEADS // N_GROUPS
D_STATE = 128
CONV_WIDTH = 4
CONV_DIM = D_SSM + 2 * N_GROUPS * D_STATE
SSD_CHUNK = 128
N_FOX_HEADS = 16
FOX_HEAD_DIM = 64
D_FOX = N_FOX_HEADS * FOX_HEAD_DIM
Q_BLOCK = 128
PLE_DIM = 256
RMS_EPS = 1e-6
SPLITS = (D_SSM, CONV_DIM, N_SSM_HEADS, D_FOX, D_FOX, D_FOX, N_FOX_HEADS, D_FOX, D_MODEL, D_MODEL)
D_IN_PROJ = sum(SPLITS)

kernel_name = 'hybrid_ssd_fox_gated_merge_step'


def rmsnorm(x, g):
    xf = x.astype(jnp.float32)
    inv = lax.rsqrt(jnp.mean(xf * xf, axis=-1, keepdims=True) + RMS_EPS)
    return (xf * inv).astype(x.dtype) * g


def split_proj(proj):
    idx = np.cumsum(SPLITS)[:-1].tolist()
    return jnp.split(proj, idx, axis=-1)


def causal_dwconv(x_ext, w, b):
    t_new = x_ext.shape[1] - (CONV_WIDTH - 1)
    out = b
    for k in range(CONV_WIDTH):
        out = out + x_ext[:, k:k + t_new] * w[k]
    return out


def ssd_chunked(xh, dt, a, bg, cg, init_state, chunk):
    bsz, t_len = xh.shape[:2]
    nc = t_len // chunk
    x = (xh * dt[..., None]).reshape(bsz, nc, chunk, N_GROUPS, HEADS_PER_GROUP, SSM_HEAD_DIM)
    da = (dt * a).astype(jnp.float32).reshape(bsz, nc, chunk, N_GROUPS, HEADS_PER_GROUP)
    da = jnp.moveaxis(da, 2, -1)
    bc = bg.reshape(bsz, nc, chunk, N_GROUPS, D_STATE)
    cc = cg.reshape(bsz, nc, chunk, N_GROUPS, D_STATE)
    a_cum = jnp.cumsum(da, axis=-1)
    causal = jnp.tril(jnp.ones((chunk, chunk), dtype=bool))
    seg = a_cum[..., :, None] - a_cum[..., None, :]
    decay_ls = jnp.exp(jnp.where(causal, seg, -jnp.inf))
    cb = jnp.einsum('bclgn,bcsgn->bcgls', cc, bc)
    y_diag = jnp.einsum('bcgls,bcgels,bcsgep->bclgep', cb, decay_ls, x)
    decay_to_end = jnp.exp(a_cum[..., -1:] - a_cum)
    chunk_states = jnp.einsum('bclgn,bcgel,bclgep->bcgepn', bc, decay_to_end, x)
    chunk_decay = jnp.exp(a_cum[..., -1])
    s0 = init_state.reshape(bsz, N_GROUPS, HEADS_PER_GROUP, SSM_HEAD_DIM, D_STATE).astype(jnp.float32)

    def step(s, inp):
        cs, cd = inp
        return s * cd[..., None, None] + cs.astype(jnp.float32), s

    final, prev = lax.scan(step, s0, (jnp.moveaxis(chunk_states, 1, 0), jnp.moveaxis(chunk_decay, 1, 0)))
    prev = jnp.moveaxis(prev, 0, 1)
    y_off = jnp.einsum('bclgn,bcgepn,bcgel->bclgep', cc, prev, jnp.exp(a_cum))
    y = (y_diag + y_off).reshape(bsz, t_len, N_SSM_HEADS, SSM_HEAD_DIM).astype(xh.dtype)
    return y, final.reshape(bsz, N_SSM_HEADS, SSM_HEAD_DIM, D_STATE).astype(init_state.dtype)


def fox_prompt(q, k, v, logf):
    bsz, t_len = q.shape[:2]
    nb = t_len // Q_BLOCK
    scale = FOX_HEAD_DIM ** -0.5
    c = jnp.cumsum(logf.astype(jnp.float32), axis=1)
    c_t = jnp.moveaxis(c, 1, 2)
    kpos = jnp.arange(t_len)
    qb = jnp.moveaxis(q.reshape(bsz, nb, Q_BLOCK, N_FOX_HEADS, FOX_HEAD_DIM), 1, 0)
    cb = jnp.moveaxis(c.reshape(bsz, nb, Q_BLOCK, N_FOX_HEADS), 1, 0)

    def block(inp):
        q_blk, c_blk, bi = inp
        s = jnp.einsum('bqhd,bkhd->bhqk', q_blk, k).astype(jnp.float32) * scale
        s = s + (jnp.moveaxis(c_blk, 1, 2)[..., :, None] - c_t[..., None, :])
        qpos = bi * Q_BLOCK + jnp.arange(Q_BLOCK)
        s = jnp.where(kpos[None, :] <= qpos[:, None], s, -jnp.inf)
        p = jax.nn.softmax(s, axis=-1)
        return jnp.einsum('bhqk,bkhd->bqhd', p.astype(v.dtype), v)

    out = lax.map(block, (qb, cb, jnp.arange(nb)))
    return jnp.moveaxis(out, 0, 1).reshape(bsz, t_len, D_FOX)


def fox_sample(q, k_new, v_new, logf_new, cache_k, cache_v, cache_logf, page_table, layer):
    dbsz, t_len = q.shape[:2]
    n_pages = page_table.shape[1]
    past = n_pages * PAGE_SIZE
    scale = FOX_HEAD_DIM ** -0.5
    kpos = jnp.arange(past + t_len)
    qpos = past + jnp.arange(t_len)
    mask = kpos[None, :] <= qpos[:, None]

    def one_seq(inp):
        pt, qs, kn, vn, lfn = inp
        kp = cache_k[layer, pt].reshape(past, N_FOX_HEADS, FOX_HEAD_DIM)
        vp = cache_v[layer, pt].reshape(past, N_FOX_HEADS, FOX_HEAD_DIM)
        lfp = cache_logf[layer, pt].reshape(past, N_FOX_HEADS)
        kk = jnp.concatenate([kp, kn.astype(kp.dtype)], axis=0)
        vv = jnp.concatenate([vp, vn.astype(vp.dtype)], axis=0)
        c = jnp.cumsum(jnp.concatenate([lfp.astype(jnp.float32), lfn.astype(jnp.float32)], axis=0), axis=0)
        s = jnp.einsum('qhd,khd->hqk', qs, kk).astype(jnp.float32) * scale
        s = s + (c[past:].T[:, :, None] - c.T[:, None, :])
        s = jnp.where(mask, s, -jnp.inf)
        p = jax.nn.softmax(s, axis=-1)
        return jnp.einsum('hqk,khd->qhd', p.astype(vv.dtype), vv)

    out = lax.map(one_seq, (page_table, q, k_new, v_new, logf_new))
    return out.reshape(dbsz, t_len, D_FOX)


def layer_group(x, p_emb, conv_hist, ssm_init, attend, g_pre, w_in, conv_w, conv_b, dt_bias, a_log,
                d_skip, g_ssm, b_fgate, w_ssm_out, w_fox_out, w_out, g_post, w_ple, w_ple_gate, g_ple):
    bsz, t_len = x.shape[:2]
    h = rmsnorm(x, g_pre)
    z, xbc, dt_raw, q, k, v, f_raw, fox_gate, gate_a, gate_b = split_proj(h @ w_in)
    xbc_ext = jnp.concatenate([conv_hist.astype(xbc.dtype), xbc], axis=1)
    new_conv = xbc_ext[:, t_len:]
    xbc = jax.nn.silu(causal_dwconv(xbc_ext, conv_w, conv_b))
    xs, bs, cs = jnp.split(xbc, [D_SSM, D_SSM + N_GROUPS * D_STATE], axis=-1)
    xh = xs.reshape(bsz, t_len, N_SSM_HEADS, SSM_HEAD_DIM)
    dt = jax.nn.softplus(dt_raw + dt_bias)
    a = -jnp.exp(a_log)
    chunk = SSD_CHUNK if t_len % SSD_CHUNK == 0 else t_len
    y_ssm, new_ssm = ssd_chunked(xh, dt, a, bs.reshape(bsz, t_len, N_GROUPS, D_STATE),
                                 cs.reshape(bsz, t_len, N_GROUPS, D_STATE), ssm_init, chunk)
    y_ssm = (y_ssm + d_skip[:, None] * xh).reshape(bsz, t_len, D_SSM)
    y_ssm = rmsnorm(y_ssm * jax.nn.silu(z), g_ssm)
    qh = q.reshape(bsz, t_len, N_FOX_HEADS, FOX_HEAD_DIM)
    kh = k.reshape(bsz, t_len, N_FOX_HEADS, FOX_HEAD_DIM)
    vh = v.reshape(bsz, t_len, N_FOX_HEADS, FOX_HEAD_DIM)
    logf = jax.nn.log_sigmoid((f_raw + b_fgate).astype(jnp.float32))
    y_fox = attend(qh, kh, vh, logf) * jax.nn.silu(fox_gate)
    m = jax.nn.sigmoid(gate_a) * (y_ssm @ w_ssm_out) + jax.nn.sigmoid(gate_b) * (y_fox @ w_fox_out)
    x = x + rmsnorm(m @ w_out, g_post)
    gate = jax.nn.sigmoid(rmsnorm(x, g_ple) @ w_ple_gate)
    x = x + gate * (p_emb @ w_ple)
    return x, kh, vh, logf, new_ssm, new_conv


def setup_inputs(seed: int = 0) -> dict:
    key = jax.random.key(seed)
    ks = jax.random.split(key, 32)
    f32 = jnp.float32
    n_pages = PAST_LEN // PAGE_SIZE
    n_phys = (DEC_BATCH * n_pages * 5) // 4

    def nrm(k, shape, scale=1.0):
        return jax.random.normal(k, shape, f32) * scale

    x_prompt = nrm(ks[0], (BATCH, SEQ, D_MODEL))
    x_sample = nrm(ks[1], (DEC_BATCH, DEC_SEQ, D_MODEL))
    cache_k = nrm(ks[2], (DEPTH, n_phys, PAGE_SIZE, N_FOX_HEADS, FOX_HEAD_DIM))
    cache_v = nrm(ks[3], (DEPTH, n_phys, PAGE_SIZE, N_FOX_HEADS, FOX_HEAD_DIM))
    cache_logf = jax.nn.log_sigmoid(3.0 + nrm(ks[4], (DEPTH, n_phys, PAGE_SIZE, N_FOX_HEADS)))
    state_ssm = nrm(ks[5], (DEPTH, DEC_BATCH, N_SSM_HEADS, SSM_HEAD_DIM, D_STATE), 0.3)
    state_conv = nrm(ks[6], (DEPTH, DEC_BATCH, CONV_WIDTH - 1, CONV_DIM))
    page_table = jax.random.permutation(ks[7], n_phys)[:DEC_BATCH * n_pages].reshape(DEC_BATCH, n_pages).astype(jnp.int32)
    p_prompt = nrm(ks[8], (DEPTH, BATCH, SEQ, PLE_DIM))
    p_sample = nrm(ks[9], (DEPTH, DEC_BATCH, DEC_SEQ, PLE_DIM))
    g_pre = 1.0 + nrm(ks[10], (DEPTH, D_MODEL), 0.05)
    w_in = nrm(ks[11], (DEPTH, D_MODEL, D_IN_PROJ), D_MODEL ** -0.5)
    conv_w = nrm(ks[12], (DEPTH, CONV_WIDTH, CONV_DIM), CONV_WIDTH ** -0.5)
    conv_b = nrm(ks[13], (DEPTH, CONV_DIM), 0.02)
    dt0 = jnp.exp(jax.random.uniform(ks[14], (DEPTH, N_SSM_HEADS), f32, math.log(1e-3), math.log(1e-1)))
    dt_bias = dt0 + jnp.log(-jnp.expm1(-dt0))
    a_log = jnp.log(jax.random.uniform(ks[15], (DEPTH, N_SSM_HEADS), f32, 1.0, 16.0))
    d_skip = 1.0 + nrm(ks[16], (DEPTH, N_SSM_HEADS), 0.1)
    g_ssm = 1.0 + nrm(ks[17], (DEPTH, D_SSM), 0.05)
    b_fgate = jax.random.uniform(ks[18], (DEPTH, N_FOX_HEADS), f32, 1.0, 5.0)
    w_ssm_out = nrm(ks[19], (DEPTH, D_SSM, D_MODEL), D_SSM ** -0.5)
    w_fox_out = nrm(ks[20], (DEPTH, D_FOX, D_MODEL), D_FOX ** -0.5)
    w_out = nrm(ks[21], (DEPTH, D_MODEL, D_MODEL), D_MODEL ** -0.5)
    g_post = 1.0 + nrm(ks[22], (DEPTH, D_MODEL), 0.05)
    w_ple = nrm(ks[23], (DEPTH, PLE_DIM, D_MODEL), PLE_DIM ** -0.5)
    w_ple_gate = nrm(ks[24], (DEPTH, D_MODEL, D_MODEL), D_MODEL ** -0.5)
    g_ple = 1.0 + nrm(ks[25], (DEPTH, D_MODEL), 0.05)
    return {'x_prompt': x_prompt, 'x_sample': x_sample, 'cache_k': cache_k, 'cache_v': cache_v,
            'cache_logf': cache_logf, 'state_ssm': state_ssm, 'state_conv': state_conv,
            'page_table': page_table, 'p_prompt': p_prompt, 'p_sample': p_sample,
            'g_pre': g_pre, 'w_in': w_in, 'conv_w': conv_w, 'conv_b': conv_b, 'dt_bias': dt_bias,
            'a_log': a_log, 'd_skip': d_skip, 'g_ssm': g_ssm, 'b_fgate': b_fgate,
            'w_ssm_out': w_ssm_out, 'w_fox_out': w_fox_out, 'w_out': w_out, 'g_post': g_post,
            'w_ple': w_ple, 'w_ple_gate': w_ple_gate, 'g_ple': g_ple}


def reference(x_prompt, x_sample, cache_k, cache_v, cache_logf, state_ssm, state_conv, page_table,
              p_prompt, p_sample, g_pre, w_in, conv_w, conv_b, dt_bias, a_log, d_skip, g_ssm,
              b_fgate, w_ssm_out, w_fox_out, w_out, g_post, w_ple, w_ple_gate, g_ple):
    xp, xs = x_prompt, x_sample
    bsz = x_prompt.shape[0]
    zero_conv = jnp.zeros((bsz, CONV_WIDTH - 1, CONV_DIM), x_prompt.dtype)
    zero_ssm = jnp.zeros((bsz, N_SSM_HEADS, SSM_HEAD_DIM, D_STATE), jnp.float32)
    outs_p = ([], [], [], [], [])
    outs_s = ([], [], [], [], [])
    for i in range(DEPTH):
        lw = (g_pre[i], w_in[i], conv_w[i], conv_b[i], dt_bias[i], a_log[i], d_skip[i], g_ssm[i],
              b_fgate[i], w_ssm_out[i], w_fox_out[i], w_out[i], g_post[i], w_ple[i], w_ple_gate[i], g_ple[i])
        xp, kp_i, vp_i, lfp_i, ssmp_i, convp_i = layer_group(xp, p_prompt[i], zero_conv, zero_ssm, fox_prompt, *lw)
        attend_s = functools.partial(fox_sample, cache_k=cache_k, cache_v=cache_v, cache_logf=cache_logf,
                                     page_table=page_table, layer=i)
        xs, ks_i, vs_i, lfs_i, ssms_i, convs_i = layer_group(xs, p_sample[i], state_conv[i], state_ssm[i], attend_s, *lw)
        for lst, arr in zip(outs_p, (kp_i, vp_i, lfp_i, ssmp_i, convp_i)):
            lst.append(arr)
        for lst, arr in zip(outs_s, (ks_i, vs_i, lfs_i, ssms_i, convs_i)):
            lst.append(arr)
    k_p, v_p, lf_p, ssm_p, conv_p = [jnp.stack(a) for a in outs_p]
    k_s, v_s, lf_s, ssm_s, conv_s = [jnp.stack(a) for a in outs_s]
    return (xp, xs, k_p, v_p, lf_p, ssm_p, conv_p, k_s, v_s, lf_s, ssm_s, conv_s)
```

```python
import functools

import jax
import jax.numpy as jnp
from jax import lax
from jax.experimental import pallas as pl
from jax.experimental.pallas import tpu as pltpu

F32 = jnp.float32
BF16 = jnp.bfloat16
RMS_EPS = 1e-6
LANES = 128
SUBLANES = 8
VMEM_LIMIT = 56 * 1024 * 1024


def _cparams(sem):
    return pltpu.CompilerParams(dimension_semantics=sem, vmem_limit_bytes=VMEM_LIMIT)


def _silu(x):
    return x * jax.nn.sigmoid(x)


def _softplus_terms(x):
    t = jnp.log1p(jnp.exp(-jnp.abs(x)))
    return jnp.maximum(x, 0.0) + t, jnp.minimum(x, 0.0) - t


def _split2(x):
    hi = x.astype(BF16)
    lo = (x - hi.astype(F32)).astype(BF16)
    return hi, lo


def _split3(x):
    hi = x.astype(BF16)
    r = x - hi.astype(F32)
    mid = r.astype(BF16)
    lo = (r - mid.astype(F32)).astype(BF16)
    return hi, mid, lo


def _dot(a, b):
    return jnp.dot(a, b, preferred_element_type=F32)


def _dot_nt(a, b):
    return lax.dot_general(a, b, (((1,), (1,)), ((), ())), preferred_element_type=F32)


_IN_TN = 512


def _inproj_kernel(x_ref, g_ref, w_ref, ws_ref,
                   z_ref, q_ref, k_ref, kb_ref, v_ref, vb_ref, fg_ref, ga_ref, gb_ref, xbc_ref, fdt_ref,
                   h_ref, *, seg_starts, q_scale):
    j = pl.program_id(1)

    @pl.when(j == 0)
    def _():
        x = x_ref[...]
        inv = lax.rsqrt(jnp.mean(x * x, axis=-1, keepdims=True) + RMS_EPS)
        hb = ((x * inv) * g_ref[...]).astype(BF16)
        h_ref[...] = hb
        fdt_ref[...] = _dot(hb, ws_ref[...])

    r = _dot(h_ref[...], w_ref[...])
    s_z, s_q, s_k, s_v, s_fg, s_ga, s_gb, s_xbc, s_end = seg_starts

    def seg(lo, hi):
        return jnp.logical_and(j >= lo, j < hi)

    @pl.when(seg(s_z, s_q))
    def _():
        z_ref[...] = r

    @pl.when(seg(s_q, s_k))
    def _():
        q_ref[...] = (r * q_scale).astype(BF16)

    @pl.when(seg(s_k, s_v))
    def _():
        k_ref[...] = r
        kb_ref[...] = r.astype(BF16)

    @pl.when(seg(s_v, s_fg))
    def _():
        v_ref[...] = r
        vb_ref[...] = r.astype(BF16)

    @pl.when(seg(s_fg, s_ga))
    def _():
        fg_ref[...] = r

    @pl.when(seg(s_ga, s_gb))
    def _():
        ga_ref[...] = r

    @pl.when(seg(s_gb, s_xbc))
    def _():
        gb_ref[...] = r

    @pl.when(seg(s_xbc, s_end))
    def _():
        xbc_ref[...] = r


def _inproj(x2, g_pre, w_main, w_small, *, d_ssm, conv_dim, d_fox, q_scale):
    m, d = x2.shape
    tn = _IN_TN
    tm = min(512, m)
    assert m % tm == 0 and d_ssm % tn == 0 and d_fox % tn == 0 and conv_dim % tn == 0 and d % tn == 0
    widths = (d_ssm, d_fox, d_fox, d_fox, d_fox, d, d, conv_dim)
    starts = [0]
    for w in widths:
        starts.append(starts[-1] + w // tn)
    nj = starts[-1]
    assert w_main.shape == (d, nj * tn)

    def seg_spec(si):
        lo, nb = starts[si], widths[si] // tn
        return pl.BlockSpec((tm, tn), lambda i, j: (i, jnp.clip(j - lo, 0, nb - 1)))

    out_shapes = [
        jax.ShapeDtypeStruct((m, d_ssm), F32),
        jax.ShapeDtypeStruct((m, d_fox), BF16),
        jax.ShapeDtypeStruct((m, d_fox), F32),
        jax.ShapeDtypeStruct((m, d_fox), BF16),
        jax.ShapeDtypeStruct((m, d_fox), F32),
        jax.ShapeDtypeStruct((m, d_fox), BF16),
        jax.ShapeDtypeStruct((m, d_fox), F32),
        jax.ShapeDtypeStruct((m, d), F32),
        jax.ShapeDtypeStruct((m, d), F32),
        jax.ShapeDtypeStruct((m, conv_dim), F32),
        jax.ShapeDtypeStruct((m, LANES), F32),
    ]
    out_specs = [seg_spec(0), seg_spec(1), seg_spec(2), seg_spec(2), seg_spec(3), seg_spec(3),
                 seg_spec(4), seg_spec(5), seg_spec(6), seg_spec(7),
                 pl.BlockSpec((tm, LANES), lambda i, j: (i, 0))]
    return pl.pallas_call(
        functools.partial(_inproj_kernel, seg_starts=tuple(starts), q_scale=q_scale),
        grid=(m // tm, nj),
        in_specs=[pl.BlockSpec((tm, d), lambda i, j: (i, 0)),
                  pl.BlockSpec((1, d), lambda i, j: (0, 0)),
                  pl.BlockSpec((d, tn), lambda i, j: (0, j)),
                  pl.BlockSpec((d, LANES), lambda i, j: (0, 0))],
        out_specs=out_specs,
        out_shape=out_shapes,
        scratch_shapes=[pltpu.VMEM((tm, d), BF16)],
        compiler_params=_cparams(("parallel", "arbitrary")),
        name="inproj",
    )(x2, g_pre.reshape(1, d), w_main, w_small)


def _ssd_kernel(xbc_ref, z_ref, fdt_ref, hist_ref, init_ref, cw_ref, cb_ref, bias_ref, alog_ref,
                dsk_ref, gssm_ref, e_ref,
                y_ref, logf_ref, nck_ref, fin_ref,
                xext_ref, st_ref, carry_ref, *, L, H, P, G, N, HF, CW):
    c = pl.program_id(1)
    nc = pl.num_programs(1)
    DS = H * P
    DSG = DS // G
    HG = H // G

    @pl.when(c == 0)
    def _():
        xext_ref[0:SUBLANES, :] = hist_ref[0]
        st_ref[...] = init_ref[0].T
        carry_ref[...] = jnp.zeros_like(carry_ref)

    xext_ref[SUBLANES:SUBLANES + L, :] = xbc_ref[...]
    cw = cw_ref[...]
    conv = cb_ref[...]
    for kk in range(CW):
        off = SUBLANES - (CW - 1) + kk
        conv = conv + xext_ref[off:off + L, :] * cw[kk:kk + 1, :]
    xext_ref[0:SUBLANES, :] = xext_ref[L:L + SUBLANES, :]
    xc = _silu(conv)
    xs = xc[:, :DS]
    bm = xc[:, DS:DS + G * N]
    cm = xc[:, DS + G * N:]

    lane = lax.broadcasted_iota(jnp.int32, (1, LANES), 1)
    is_f = lane < HF
    is_dt = jnp.logical_and(lane >= HF, lane < HF + H)
    sp, lsg = _softplus_terms(fdt_ref[...] + bias_ref[...])
    dt = jnp.where(is_dt, sp, 0.0)
    da = dt * (-jnp.exp(alog_ref[...]))
    lf = jnp.where(is_f, lsg, 0.0)
    rowi = lax.broadcasted_iota(jnp.int32, (L, 1), 0)
    comb = da + lf + jnp.where(rowi == 0, carry_ref[...], 0.0)

    r_i = lax.broadcasted_iota(jnp.int32, (L, L), 0)
    c_i = lax.broadcasted_iota(jnp.int32, (L, L), 1)
    causal = r_i >= c_i
    tril = jnp.where(causal, 1.0, 0.0).astype(BF16)
    hi, mid, lo = _split3(comb)
    cum3 = _dot(tril, jnp.concatenate([hi, mid, lo], axis=1))
    cum = cum3[:, :LANES] + cum3[:, LANES:2 * LANES] + cum3[:, 2 * LANES:]
    carry_ref[...] = jnp.where(is_f, cum[L - 1:L, :], 0.0)
    cum_t = cum.T
    logf_ref[...] = lsg[:, :HF]
    nck_ref[0] = -cum_t[0:HF, :]

    a_last = cum[L - 1:L, :]
    qq = jnp.concatenate([dt, jnp.where(is_dt, jnp.exp(cum), 0.0),
                          jnp.where(is_dt, jnp.exp(a_last - cum), 0.0)], axis=0)
    qhi, qlo = _split2(qq)
    ex = _dot(jnp.concatenate([qhi, qlo], axis=0), e_ref[...])
    ex = ex[:3 * L] + ex[3 * L:]
    dt_e = ex[0:L]
    ea_e = ex[L:2 * L]
    dte_e = ex[2 * L:3 * L]

    xdt = xs * dt_e
    xdte_b = (xdt * dte_e).astype(BF16)
    lane_p = lax.broadcasted_iota(jnp.int32, (1, 2 * P), 1)
    first_head = lane_p < P

    y_groups = []
    for g in range(G):
        gs = slice(g * DSG, (g + 1) * DSG)
        bg_t = bm[:, g * N:(g + 1) * N].T.astype(BF16)
        cg = cm[:, g * N:(g + 1) * N].astype(BF16)
        cb = _dot(cg, bg_t)
        s_t = st_ref[:, gs]
        y_off = _dot(cg, s_t.astype(BF16)) * ea_e[:, gs]
        y_pairs = []
        for jp in range(HG // 2):
            e0 = g * HG + 2 * jp
            ms = []
            for e in (e0, e0 + 1):
                col = HF + e
                seg = cum[:, col:col + 1] - cum_t[col:col + 1, :]
                ms.append((cb * jnp.exp(jnp.where(causal, seg, -jnp.inf))).astype(BF16))
            lhs = jnp.concatenate(ms, axis=1)
            xp = xdt[:, e0 * P:(e0 + 2) * P]
            rhs = jnp.concatenate([jnp.where(first_head, xp, 0.0).astype(BF16),
                                   jnp.where(first_head, 0.0, xp).astype(BF16)], axis=0)
            y_pairs.append(_dot(lhs, rhs))
        y_groups.append(jnp.concatenate(y_pairs, axis=1) + y_off)
        st_ref[:, gs] = s_t * ea_e[L - 1:L, gs] + _dot(bg_t, xdte_b[:, gs])
    y = jnp.concatenate(y_groups, axis=1) + dsk_ref[...] * xs

    yg = y * _silu(z_ref[...])
    inv = lax.rsqrt(jnp.mean(yg * yg, axis=-1, keepdims=True) + RMS_EPS)
    y_ref[...] = ((yg * inv) * gssm_ref[...]).astype(BF16)

    @pl.when(c == nc - 1)
    def _():
        fin_ref[0] = st_ref[...].T


def _ssd_prompt(xbc, z, fdt, hist8, init, conv_w, conv_b, bias128, alog128, dsk_e, g_ssm, e_mat,
                *, B, T, L, H, P, G, N, HF):
    DS = H * P
    cd = xbc.shape[1]
    CW = conv_w.shape[0]
    nc = T // L
    assert T % L == 0 and L == LANES and 2 * P == LANES and H % (2 * G) == 0 and HF + H <= LANES
    kern = functools.partial(_ssd_kernel, L=L, H=H, P=P, G=G, N=N, HF=HF, CW=CW)
    tok = lambda b, c: (b * nc + c, 0)
    const2 = lambda b, c: (0, 0)
    return pl.pallas_call(
        kern,
        grid=(B, nc),
        in_specs=[pl.BlockSpec((L, cd), tok),
                  pl.BlockSpec((L, DS), tok),
                  pl.BlockSpec((L, LANES), tok),
                  pl.BlockSpec((1, SUBLANES, cd), lambda b, c: (b, 0, 0)),
                  pl.BlockSpec((1, DS, N), lambda b, c: (b, 0, 0)),
                  pl.BlockSpec((CW, cd), const2),
                  pl.BlockSpec((1, cd), const2),
                  pl.BlockSpec((1, LANES), const2),
                  pl.BlockSpec((1, LANES), const2),
                  pl.BlockSpec((1, DS), const2),
                  pl.BlockSpec((1, DS), const2),
                  pl.BlockSpec((LANES, DS), const2)],
        out_specs=[pl.BlockSpec((L, DS), tok),
                   pl.BlockSpec((L, HF), tok),
                   pl.BlockSpec((1, HF, L), lambda b, c: (b, 0, c)),
                   pl.BlockSpec((1, DS, N), lambda b, c: (b, 0, 0))],
        out_shape=[jax.ShapeDtypeStruct((B * T, DS), BF16),
                   jax.ShapeDtypeStruct((B * T, HF), F32),
                   jax.ShapeDtypeStruct((B, HF, T), F32),
                   jax.ShapeDtypeStruct((B, DS, N), F32)],
        scratch_shapes=[pltpu.VMEM((L + SUBLANES, cd), F32),
                        pltpu.VMEM((N, DS), F32),
                        pltpu.VMEM((1, LANES), F32)],
        compiler_params=_cparams(("arbitrary", "arbitrary")),
        name="ssd_prompt",
    )(xbc, z, fdt, hist8, init, conv_w, conv_b, bias128, alog128, dsk_e, g_ssm, e_mat)


_ATT_TQ = 512


def _attn_kernel(q_ref, k_ref, v_ref, nck_ref, fg_ref, o_ref, *, tq, dh):
    qi = pl.program_id(2)
    q = q_ref[...]
    lane = lax.broadcasted_iota(jnp.int32, (1, 2 * dh), 1)
    first = lane < dh
    qf = q.astype(F32)
    q_heads = (jnp.where(first, qf, 0.0).astype(BF16), jnp.where(first, 0.0, qf).astype(BF16))

    def step(ki, carry, mask):
        off = pl.multiple_of(ki * tq, tq)
        kb = k_ref[pl.ds(off, tq), :]
        vb = v_ref[pl.ds(off, tq), :]
        out = []
        for h in range(2):
            m, l, acc = carry[h]
            s = _dot_nt(q_heads[h], kb) + nck_ref[0, h, pl.ds(ki, 1), :]
            if mask is not None:
                s = jnp.where(mask, s, -jnp.inf)
            m_new = jnp.maximum(m, jnp.max(s, axis=-1, keepdims=True))
            alpha = jnp.exp(m - m_new)
            p = jnp.exp(s - m_new)
            l = alpha * l + jnp.sum(p, axis=-1, keepdims=True)
            acc = alpha * acc + _dot(p.astype(BF16), vb)
            out.append((m_new, l, acc))
        return tuple(out)

    init = tuple((jnp.full((tq, 1), -jnp.inf, F32), jnp.zeros((tq, 1), F32), jnp.zeros((tq, 2 * dh), F32))
                 for _ in range(2))
    carry = lax.fori_loop(0, qi, lambda ki, cr: step(ki, cr, None), init)
    r_i = lax.broadcasted_iota(jnp.int32, (tq, tq), 0)
    c_i = lax.broadcasted_iota(jnp.int32, (tq, tq), 1)
    carry = step(qi, carry, c_i <= r_i)
    (_, l0, a0), (_, l1, a1) = carry
    o = jnp.where(first, a0 / l0, a1 / l1)
    o_ref[...] = (o * _silu(fg_ref[...])).astype(BF16)


def _attn_prompt(q, kb, vb, nck, fg, *, B, T, HF, dh):
    tq = min(_ATT_TQ, T)
    nq = T // tq
    assert T % tq == 0 and 2 * dh == LANES and HF % 2 == 0
    npair = HF // 2
    nck4 = nck.reshape(B, HF, nq, tq)
    return pl.pallas_call(
        functools.partial(_attn_kernel, tq=tq, dh=dh),
        grid=(B, npair, nq),
        in_specs=[pl.BlockSpec((tq, LANES), lambda b, hp, qi: (b * nq + qi, hp)),
                  pl.BlockSpec((T, LANES), lambda b, hp, qi: (b, hp)),
                  pl.BlockSpec((T, LANES), lambda b, hp, qi: (b, hp)),
                  pl.BlockSpec((1, 2, nq, tq), lambda b, hp, qi: (b, hp, 0, 0)),
                  pl.BlockSpec((tq, LANES), lambda b, hp, qi: (b * nq + qi, hp))],
        out_specs=pl.BlockSpec((tq, LANES), lambda b, hp, qi: (b * nq + qi, hp)),
        out_shape=jax.ShapeDtypeStruct((B * T, HF * dh), BF16),
        compiler_params=_cparams(("parallel", "parallel", "arbitrary")),
        name="fox_prompt",
    )(q, kb, vb, nck4, fg)


def _sample_pre_kernel(xbc_ref, sconv_ref, fdt_ref, cw_ref, cb_ref, bias_ref, alog_ref, e_ref,
                       logf_ref, xs_ref, xdt_t_ref, ed_t_ref, bs_ref, cs_ref, nconv_ref,
                       *, H, P, G, N, HF, CW):
    DS = H * P
    cd = xbc_ref.shape[1]
    x = xbc_ref[...]
    rows = [sconv_ref[:, kk * cd:(kk + 1) * cd] for kk in range(CW - 1)] + [x]
    cw = cw_ref[...]
    conv = cb_ref[...]
    for kk in range(CW):
        conv = conv + rows[kk] * cw[kk:kk + 1, :]
    nconv_ref[...] = jnp.concatenate(rows[1:], axis=1)
    xc = _silu(conv)
    xs = xc[:, :DS]
    xs_ref[...] = xs
    bs_ref[...] = xc[:, DS:DS + G * N]
    cs_ref[...] = xc[:, DS + G * N:]

    lane = lax.broadcasted_iota(jnp.int32, (1, LANES), 1)
    is_dt = jnp.logical_and(lane >= HF, lane < HF + H)
    sp, lsg = _softplus_terms(fdt_ref[...] + bias_ref[...])
    logf_ref[...] = lsg[:, :HF]
    dt = jnp.where(is_dt, sp, 0.0)
    ed = jnp.where(is_dt, jnp.exp(dt * (-jnp.exp(alog_ref[...]))), 0.0)
    s = dt.shape[0]
    hi, mid, lo = _split3(jnp.concatenate([dt, ed], axis=0))
    ex = _dot(jnp.concatenate([hi, mid, lo], axis=0), e_ref[...])
    ex = ex[:2 * s] + ex[2 * s:4 * s] + ex[4 * s:]
    xdt_t_ref[...] = (xs * ex[:s]).T
    ed_t_ref[...] = ex[s:].T


def _sample_pre(xbc, sconv, fdt, conv_w, conv_b, bias128, alog128, e_mat, *, H, P, G, N, HF):
    s, cd = xbc.shape
    DS = H * P
    CW = conv_w.shape[0]
    kern = functools.partial(_sample_pre_kernel, H=H, P=P, G=G, N=N, HF=HF, CW=CW)
    return pl.pallas_call(
        kern,
        out_shape=[jax.ShapeDtypeStruct((s, HF), F32),
                   jax.ShapeDtypeStruct((s, DS), F32),
                   jax.ShapeDtypeStruct((DS, s), F32),
                   jax.ShapeDtypeStruct((DS, s), F32),
                   jax.ShapeDtypeStruct((s, G * N), F32),
                   jax.ShapeDtypeStruct((s, G * N), F32),
                   jax.ShapeDtypeStruct((s, (CW - 1) * cd), F32)],
        compiler_params=pltpu.CompilerParams(vmem_limit_bytes=VMEM_LIMIT),
        name="sample_pre",
    )(xbc, sconv, fdt, conv_w, conv_b, bias128, alog128, e_mat)


def _sample_state_kernel(s0_ref, xdt_t_ref, ed_t_ref, bs_ref, cs_ref, xs_ref, z_ref, dsk_ref, gssm_ref,
                         new_ref, y_ref, *, H, P, G, N):
    si = pl.program_id(0)
    DS = H * P
    DSG = DS // G
    ns = xdt_t_ref.shape[1]
    lane = lax.broadcasted_iota(jnp.int32, (1, ns), 1)
    pick = lane == si
    xcol = jnp.sum(jnp.where(pick, xdt_t_ref[...], 0.0), axis=1, keepdims=True)
    ecol = jnp.sum(jnp.where(pick, ed_t_ref[...], 0.0), axis=1, keepdims=True)
    bs = bs_ref[0]
    cs = cs_ref[0]
    ys = []
    for g in range(G):
        gs = slice(g * DSG, (g + 1) * DSG)
        new_g = s0_ref[0, gs, :] * ecol[gs] + xcol[gs] * bs[:, g * N:(g + 1) * N]
        new_ref[0, gs, :] = new_g
        chi, clo = _split2(cs[:, g * N:(g + 1) * N])
        cst = jnp.concatenate([chi, clo, jnp.zeros((SUBLANES - 2, N), BF16)], axis=0)
        nhi, nlo = _split2(new_g)
        r = _dot_nt(cst, nhi) + _dot_nt(cst, nlo)
        ys.append(r[0:1] + r[1:2])
    y = jnp.concatenate(ys, axis=1) + dsk_ref[...] * xs_ref[0]
    yg = y * _silu(z_ref[0])
    inv = lax.rsqrt(jnp.mean(yg * yg, axis=-1, keepdims=True) + RMS_EPS)
    y_ref[0] = ((yg * inv) * gssm_ref[...]).astype(BF16)


def _sample_state(s0, xdt_t, ed_t, bs, cs, xs, z, dsk_e, g_ssm, *, H, P, G, N):
    ns, DS, _ = s0.shape
    row3 = lambda a: a.reshape(ns, 1, a.shape[-1])
    per = lambda w: pl.BlockSpec((1, 1, w), lambda s: (s, 0, 0))
    const2 = lambda s: (0, 0)
    new, y = pl.pallas_call(
        functools.partial(_sample_state_kernel, H=H, P=P, G=G, N=N),
        grid=(ns,),
        in_specs=[pl.BlockSpec((1, DS, N), lambda s: (s, 0, 0)),
                  pl.BlockSpec((DS, ns), const2),
                  pl.BlockSpec((DS, ns), const2),
                  per(G * N), per(G * N), per(DS), per(DS),
                  pl.BlockSpec((1, DS), const2),
                  pl.BlockSpec((1, DS), const2)],
        out_specs=[pl.BlockSpec((1, DS, N), lambda s: (s, 0, 0)), per(DS)],
        out_shape=[jax.ShapeDtypeStruct((ns, DS, N), F32),
                   jax.ShapeDtypeStruct((ns, 1, DS), BF16)],
        compiler_params=_cparams(("parallel",)),
        name="sample_state",
    )(s0, xdt_t, ed_t, row3(bs), row3(cs), row3(xs), row3(z), dsk_e, g_ssm)
    return new, y.reshape(ns, DS)


_PAGES_PER_STEP = 8


def _paged_kernel(pt_ref, q_ref, kn_ref, vn_ref, lfn_ref, fg_ref, ind_ref, *rest, PP, HF, DF):
    k_refs = rest[0:PP]
    v_refs = rest[PP:2 * PP]
    lf_refs = rest[2 * PP:3 * PP]
    o_ref = rest[3 * PP]
    m_ref, l_ref, r_ref, acc_ref, qrow_ref = rest[3 * PP + 1:]
    j = pl.program_id(1)
    nj = pl.num_programs(1)
    ps = k_refs[0].shape[1]
    ind = ind_ref[...]
    e_b = ind.astype(BF16)

    def expand(row):
        hi, mid, lo = _split3(row)
        a = jnp.concatenate([hi, mid, lo, jnp.zeros((SUBLANES - 3, HF), BF16)], axis=0)
        r = _dot(a, e_b)
        return r[0:1] + r[1:2] + r[2:3]

    @pl.when(j == 0)
    def _():
        qrow = (ind * q_ref[0].astype(F32)).astype(BF16)
        qrow_ref[...] = qrow
        kn = jnp.broadcast_to(kn_ref[0], (SUBLANES, DF))
        m_ref[...] = _dot_nt(kn, qrow)[0:1]
        l_ref[...] = jnp.ones_like(l_ref)
        r_ref[...] = lfn_ref[0]
        rowi = lax.broadcasted_iota(jnp.int32, (SUBLANES, 1), 0)
        acc_ref[...] = jnp.where(rowi == 0, vn_ref[0].astype(F32), 0.0)

    qrow = qrow_ref[...]
    r_i = lax.broadcasted_iota(jnp.int32, (ps, ps), 0)
    c_i = lax.broadcasted_iota(jnp.int32, (ps, ps), 1)
    later = jnp.where(c_i > r_i, 1.0, 0.0).astype(BF16)
    kall = jnp.concatenate([k_refs[r][0].astype(BF16) for r in range(PP)], axis=0)
    s = _dot_nt(kall, qrow)
    rcur = r_ref[...]
    biases = []
    for r in range(PP):
        lf = lf_refs[r][0]
        hi, mid, lo = _split3(lf)
        biases.append(_dot(later, hi) + _dot(later, mid) + _dot(later, lo) + rcur)
        rcur = rcur + jnp.sum(lf, axis=0, keepdims=True)
    r_ref[...] = rcur
    s = s + jnp.concatenate(biases, axis=0)
    m_old = m_ref[...]
    m_new = jnp.maximum(m_old, jnp.max(s, axis=0, keepdims=True))
    alpha = jnp.exp(m_old - m_new)
    p = jnp.exp(s - m_new)
    l_ref[...] = alpha * l_ref[...] + jnp.sum(p, axis=0, keepdims=True)
    m_ref[...] = m_new
    pe = _dot(p.astype(BF16), e_b)
    contrib = jnp.zeros((SUBLANES, DF), F32)
    for r in range(PP):
        pv = pe[r * ps:(r + 1) * ps] * v_refs[r][0]
        contrib = contrib + jnp.sum(pv.reshape(ps // SUBLANES, SUBLANES, DF), axis=0)
    acc_ref[...] = expand(alpha) * acc_ref[...] + contrib

    @pl.when(j == nj - 1)
    def _():
        o = jnp.sum(acc_ref[...], axis=0, keepdims=True) / expand(l_ref[...])
        o_ref[0] = (o * _silu(fg_ref[0])).astype(BF16)


def _paged_attn(page_idx, q, kn, vn, lfn, fg, ind, ck, cv, clf, *, HF, DF):
    ns, npages = page_idx.shape
    ps = ck.shape[1]
    PP = min(_PAGES_PER_STEP, npages)
    assert npages % PP == 0 and ps % SUBLANES == 0
    nj = npages // PP
    row3 = lambda a: a.reshape(ns, 1, a.shape[-1])
    per = lambda w: pl.BlockSpec((1, 1, w), lambda s, j, pt: (s, 0, 0))

    def page_spec(r, w):
        return pl.BlockSpec((1, ps, w), lambda s, j, pt: (pt[s, npages - 1 - (j * PP + r)], 0, 0))

    grid_spec = pltpu.PrefetchScalarGridSpec(
        num_scalar_prefetch=1,
        grid=(ns, nj),
        in_specs=[per(DF), per(DF), per(DF), per(HF), per(DF),
                  pl.BlockSpec((HF, DF), lambda s, j, pt: (0, 0))]
                 + [page_spec(r, DF) for r in range(PP)]
                 + [page_spec(r, DF) for r in range(PP)]
                 + [page_spec(r, HF) for r in range(PP)],
        out_specs=per(DF),
        scratch_shapes=[pltpu.VMEM((1, HF), F32), pltpu.VMEM((1, HF), F32), pltpu.VMEM((1, HF), F32),
                        pltpu.VMEM((SUBLANES, DF), F32), pltpu.VMEM((HF, DF), BF16)],
    )
    out = pl.pallas_call(
        functools.partial(_paged_kernel, PP=PP, HF=HF, DF=DF),
        grid_spec=grid_spec,
        out_shape=jax.ShapeDtypeStruct((ns, 1, DF), BF16),
        compiler_params=_cparams(("parallel", "arbitrary")),
        name="fox_paged",
    )(page_idx, row3(q), row3(kn), row3(vn), row3(lfn), row3(fg), ind,
      *([ck] * PP), *([cv] * PP), *([clf] * PP))
    return out.reshape(ns, DF)


def _final_kernel(ys_ref, yf_ref, ga_ref, gb_ref, x_ref, pe_ref,
                  wso_ref, wfo_ref, wo_ref, wpg_ref, wple_ref, gpost_ref, gple_ref, o_ref):
    a = _dot(ys_ref[...], wso_ref[...])
    b = _dot(yf_ref[...], wfo_ref[...])
    m = jax.nn.sigmoid(ga_ref[...]) * a + jax.nn.sigmoid(gb_ref[...]) * b
    o = _dot(m.astype(BF16), wo_ref[...])
    inv = lax.rsqrt(jnp.mean(o * o, axis=-1, keepdims=True) + RMS_EPS)
    x1 = x_ref[...] + (o * inv) * gpost_ref[...]
    inv1 = lax.rsqrt(jnp.mean(x1 * x1, axis=-1, keepdims=True) + RMS_EPS)
    hg = ((x1 * inv1) * gple_ref[...]).astype(BF16)
    gate = jax.nn.sigmoid(_dot(hg, wpg_ref[...]))
    pe = _dot(pe_ref[...].astype(BF16), wple_ref[...])
    o_ref[...] = x1 + gate * pe


def _final(ys, yf, ga, gb, x2, pemb, wso, wfo, wo, wpg, wple, g_post, g_ple):
    m, d = x2.shape
    tm = min(512, m)
    assert m % tm == 0
    tok = lambda w: pl.BlockSpec((tm, w), lambda i: (i, 0))
    full = lambda a: pl.BlockSpec(a.shape, lambda i: (0, 0))
    gp = g_post.reshape(1, d)
    gl = g_ple.reshape(1, d)
    return pl.pallas_call(
        _final_kernel,
        grid=(m // tm,),
        in_specs=[tok(ys.shape[1]), tok(yf.shape[1]), tok(d), tok(d), tok(d), tok(pemb.shape[1]),
                  full(wso), full(wfo), full(wo), full(wpg), full(wple), full(gp), full(gl)],
        out_specs=tok(d),
        out_shape=jax.ShapeDtypeStruct((m, d), F32),
        compiler_params=_cparams(("parallel",)),
        name="merge_out",
    )(ys, yf, ga, gb, x2, pemb, wso, wfo, wo, wpg, wple, gp, gl)


def kernel(x_prompt, x_sample, cache_k, cache_v, cache_logf, state_ssm, state_conv, page_table, p_prompt, p_sample, g_pre, w_in, conv_w, conv_b, dt_bias, a_log, d_skip, g_ssm, b_fgate, w_ssm_out, w_fox_out, w_out, g_post, w_ple, w_ple_gate, g_ple):
    B, T, D = x_prompt.shape
    S, TS, _ = x_sample.shape
    assert TS == 1
    depth = w_in.shape[0]
    _, n_phys, ps, HF, dh = cache_k.shape
    _, _, H, P, N = state_ssm.shape
    CW, cd = conv_w.shape[1], conv_w.shape[2]
    DS = H * P
    DF = HF * dh
    G = (cd - DS) // (2 * N)
    L = LANES
    q_scale = float(dh) ** -0.5
    pad_l = LANES - HF - H

    ch_s = jnp.arange(DS) // P
    e_ssm = (jnp.arange(LANES)[:, None] == (HF + ch_s)[None, :]).astype(BF16)
    ind_fox = (jnp.arange(HF)[:, None] == (jnp.arange(DF) // dh)[None, :]).astype(F32)

    ck = cache_k.reshape(depth * n_phys, ps, DF)
    cv = cache_v.reshape(depth * n_phys, ps, DF)
    clf = cache_logf.reshape(depth * n_phys, ps, HF)

    xp = x_prompt.reshape(B * T, D)
    xs_tok = x_sample.reshape(S, D)
    outs_p = ([], [], [], [], [])
    outs_s = ([], [], [], [], [])
    for i in range(depth):
        z_w, xbc_w, dt_w, q_w, k_w, v_w, f_w, fg_w, ga_w, gb_w = jnp.split(
            w_in[i], [DS, DS + cd, DS + cd + H, DS + cd + H + DF, DS + cd + H + 2 * DF,
                      DS + cd + H + 3 * DF, DS + cd + H + 3 * DF + HF, DS + cd + H + 4 * DF + HF,
                      DS + cd + H + 4 * DF + HF + D], axis=1)
        w_main = jnp.concatenate([z_w, q_w, k_w, v_w, fg_w, ga_w, gb_w, xbc_w], axis=1).astype(BF16)
        w_small = jnp.concatenate([f_w, dt_w, jnp.zeros((D, pad_l), F32)], axis=1).astype(BF16)
        bias128 = jnp.concatenate([b_fgate[i], dt_bias[i], jnp.zeros((pad_l,), F32)]).reshape(1, LANES)
        alog128 = jnp.concatenate([jnp.zeros((HF,), F32), a_log[i], jnp.zeros((pad_l,), F32)]).reshape(1, LANES)
        dsk_e = jnp.repeat(d_skip[i], P).reshape(1, DS)
        gssm = g_ssm[i].reshape(1, DS)
        cw_i = conv_w[i]
        cb_i = conv_b[i].reshape(1, cd)
        wso = w_ssm_out[i].astype(BF16)
        wfo = w_fox_out[i].astype(BF16)
        wo = w_out[i].astype(BF16)
        wpg = w_ple_gate[i].astype(BF16)
        wple = w_ple[i].astype(BF16)
        proj = functools.partial(_inproj, g_pre=g_pre[i], w_main=w_main, w_small=w_small,
                                 d_ssm=DS, conv_dim=cd, d_fox=DF, q_scale=q_scale)
        fin = functools.partial(_final, wso=wso, wfo=wfo, wo=wo, wpg=wpg, wple=wple,
                                g_post=g_post[i], g_ple=g_ple[i])

        z, q, k, kb, v, vb, fg, ga, gb, xbc, fdt = proj(xp)
        hist8 = jnp.zeros((B, SUBLANES, cd), F32)
        init = jnp.zeros((B, DS, N), F32)
        y_ssm, logf, nck, ssm_fin = _ssd_prompt(xbc, z, fdt, hist8, init, cw_i, cb_i, bias128, alog128,
                                                dsk_e, gssm, e_ssm, B=B, T=T, L=L, H=H, P=P, G=G, N=N, HF=HF)
        y_fox = _attn_prompt(q, kb, vb, nck, fg, B=B, T=T, HF=HF, dh=dh)
        xp = fin(y_ssm, y_fox, ga, gb, xp, p_prompt[i].reshape(B * T, -1))
        xbc3 = xbc.reshape(B, T, cd)
        hist_p = jnp.zeros((B, CW - 1, cd), F32)
        conv_p = jnp.concatenate([hist_p, xbc3], axis=1)[:, T:] if T < CW - 1 else xbc3[:, T - (CW - 1):]
        for lst, arr in zip(outs_p, (k.reshape(B, T, HF, dh), v.reshape(B, T, HF, dh), logf.reshape(B, T, HF),
                                     ssm_fin.reshape(B, H, P, N), conv_p)):
            lst.append(arr)

        z, q, k, kb, v, vb, fg, ga, gb, xbc, fdt = proj(xs_tok)
        logf_s, xs_c, xdt_t, ed_t, bs, cs, nconv = _sample_pre(
            xbc, state_conv[i].reshape(S, (CW - 1) * cd), fdt, cw_i, cb_i, bias128, alog128, e_ssm,
            H=H, P=P, G=G, N=N, HF=HF)
        ssm_new, y_ssm = _sample_state(state_ssm[i].reshape(S, DS, N), xdt_t, ed_t, bs, cs, xs_c, z,
                                       dsk_e, gssm, H=H, P=P, G=G, N=N)
        y_fox = _paged_attn(page_table + i * n_phys, q, kb, vb, logf_s, fg, ind_fox, ck, cv, clf, HF=HF, DF=DF)
        xs_tok = fin(y_ssm, y_fox, ga, gb, xs_tok, p_sample[i].reshape(S, -1))
        for lst, arr in zip(outs_s, (k.reshape(S, 1, HF, dh), v.reshape(S, 1, HF, dh), logf_s.reshape(S, 1, HF),
                                     ssm_new.reshape(S, H, P, N), nconv.reshape(S, CW - 1, cd))):
            lst.append(arr)

    k_p, v_p, lf_p, ssm_p, conv_p = [jnp.stack(a) for a in outs_p]
    k_s, v_s, lf_s, ssm_s, conv_s = [jnp.stack(a) for a in outs_s]
    return (xp.reshape(B, T, D), xs_tok.reshape(S, 1, D), k_p, v_p, lf_p, ssm_p, conv_p,
            k_s, v_s, lf_s, ssm_s, conv_s)
```

```python
import functools
import math

import jax
import jax.numpy as jnp
from jax import lax
from jax.experimental import pallas as pl
from jax.experimental.pallas import tpu as pltpu

F32 = jnp.float32
BF16 = jnp.bfloat16
RMS_EPS = 1e-6
LOG2E = math.log2(math.e)
LANES = 128
SUBLANES = 8
VMEM_LIMIT = 56 * 1024 * 1024


def _cparams(sem):
    return pltpu.CompilerParams(dimension_semantics=sem, vmem_limit_bytes=VMEM_LIMIT)


def _silu(x):
    return x * jax.nn.sigmoid(x)


def _softplus_terms(x):
    t = jnp.log1p(jnp.exp(-jnp.abs(x)))
    return jnp.maximum(x, 0.0) + t, jnp.minimum(x, 0.0) - t


def _split2(x):
    hi = x.astype(BF16)
    lo = (x - hi.astype(F32)).astype(BF16)
    return hi, lo


def _split3(x):
    hi = x.astype(BF16)
    r = x - hi.astype(F32)
    mid = r.astype(BF16)
    lo = (r - mid.astype(F32)).astype(BF16)
    return hi, mid, lo


def _dot(a, b):
    return jnp.dot(a, b, preferred_element_type=F32)


def _dot_nt(a, b):
    return lax.dot_general(a, b, (((1,), (1,)), ((), ())), preferred_element_type=F32)


def _prenorm(x_ref, g_ref):
    x = x_ref[...]
    inv = lax.rsqrt(jnp.mean(x * x, axis=-1, keepdims=True) + RMS_EPS)
    return ((x * inv) * g_ref[...]).astype(BF16)


_IN_TN = 512
_IN_TM = 1024
_ATT_T = 512


def _inproj_tok_kernel(x_ref, g_ref, w_ref, ws_ref, z_ref, fg_ref, ga_ref, gb_ref, xbc_ref, fdt_ref,
                       h_ref, *, seg_starts):
    j = pl.program_id(1)

    @pl.when(j == 0)
    def _():
        hb = _prenorm(x_ref, g_ref)
        h_ref[...] = hb
        fdt_ref[...] = _dot(hb, ws_ref[...])

    outs = (z_ref, fg_ref, ga_ref, gb_ref, xbc_ref)
    for si, o_ref in enumerate(outs):
        @pl.when(jnp.logical_and(j >= seg_starts[si], j < seg_starts[si + 1]))
        def _(o_ref=o_ref):
            o_ref[...] = _dot(h_ref[...], w_ref[...])


def _inproj_tok(x2, g_pre, w_tok, w_small, *, widths):
    m, d = x2.shape
    tn = _IN_TN
    tm = min(_IN_TM, m)
    assert m % tm == 0 and all(w % tn == 0 for w in widths)
    starts = [0]
    for w in widths:
        starts.append(starts[-1] + w // tn)
    nj = starts[-1]
    assert w_tok.shape == (d, nj * tn)

    def seg_spec(si):
        lo, nb = starts[si], widths[si] // tn
        return pl.BlockSpec((tm, tn), lambda i, j: (i, jnp.clip(j - lo, 0, nb - 1)))

    return pl.pallas_call(
        functools.partial(_inproj_tok_kernel, seg_starts=tuple(starts)),
        grid=(m // tm, nj),
        in_specs=[pl.BlockSpec((tm, d), lambda i, j: (i, 0)),
                  pl.BlockSpec((1, d), lambda i, j: (0, 0)),
                  pl.BlockSpec((d, tn), lambda i, j: (0, j)),
                  pl.BlockSpec((d, LANES), lambda i, j: (0, 0))],
        out_specs=[seg_spec(si) for si in range(len(widths))]
                  + [pl.BlockSpec((tm, LANES), lambda i, j: (i, 0))],
        out_shape=[jax.ShapeDtypeStruct((m, w), F32) for w in widths]
                  + [jax.ShapeDtypeStruct((m, LANES), F32)],
        scratch_shapes=[pltpu.VMEM((tm, d), BF16)],
        compiler_params=_cparams(("parallel", "arbitrary")),
        name="inproj_tok",
    )(x2, g_pre.reshape(1, d), w_tok, w_small)


def _inproj_qkv_kernel(x_ref, g_ref, w_ref, qb_ref, qt_ref, kb_ref, kt_ref, vb_ref, vt_ref, vtb_ref,
                       h_ref, *, nb, nch, tch, q_scale):
    j = pl.program_id(2)

    @pl.when(j == 0)
    def _():
        h_ref[...] = _prenorm(x_ref, g_ref)

    @pl.when(j < nb)
    def _():
        r = _dot(h_ref[...], w_ref[...]) * q_scale
        qb_ref[...] = r.astype(BF16)
        rt = r.T.astype(BF16)
        for c in range(nch):
            qt_ref[0, c] = rt[:, c * tch:(c + 1) * tch]

    @pl.when(jnp.logical_and(j >= nb, j < 2 * nb))
    def _():
        r = _dot(h_ref[...], w_ref[...])
        kb_ref[...] = r.astype(BF16)
        kt_ref[0] = r.T

    @pl.when(j >= 2 * nb)
    def _():
        r = _dot(h_ref[...], w_ref[...])
        vb_ref[...] = r.astype(BF16)
        rt = r.T
        vt_ref[0] = rt
        rtb = rt.astype(BF16)
        for c in range(nch):
            vtb_ref[0, c] = rtb[:, c * tch:(c + 1) * tch]


def _inproj_qkv(x2, g_pre, w_qkv, *, B, T, d_fox, q_scale):
    m, d = x2.shape
    tn = _IN_TN
    tch = min(_ATT_T, T)
    tm = min(_IN_TM, T)
    assert T % tm == 0 and tm % tch == 0 and d_fox % tn == 0 and m == B * T
    nt = T // tm
    nch = tm // tch
    nb = d_fox // tn
    assert w_qkv.shape == (d, 3 * d_fox)

    def tok_spec(seg):
        return pl.BlockSpec((tm, tn), lambda b, t, j: (b * nt + t, jnp.clip(j - seg * nb, 0, nb - 1)))

    def tr_spec(seg):
        return pl.BlockSpec((1, tn, tm), lambda b, t, j: (b, jnp.clip(j - seg * nb, 0, nb - 1), t))

    def trc_spec(seg):
        return pl.BlockSpec((1, nch, tn, tch), lambda b, t, j: (b, t, jnp.clip(j - seg * nb, 0, nb - 1), 0))

    tok_shape = jax.ShapeDtypeStruct((m, d_fox), BF16)
    tr_shape = jax.ShapeDtypeStruct((B, d_fox, T), F32)
    trc_shape = jax.ShapeDtypeStruct((B, T // tch, d_fox, tch), BF16)
    return pl.pallas_call(
        functools.partial(_inproj_qkv_kernel, nb=nb, nch=nch, tch=tch, q_scale=q_scale),
        grid=(B, nt, 3 * nb),
        in_specs=[pl.BlockSpec((tm, d), lambda b, t, j: (b * nt + t, 0)),
                  pl.BlockSpec((1, d), lambda b, t, j: (0, 0)),
                  pl.BlockSpec((d, tn), lambda b, t, j: (0, j))],
        out_specs=[tok_spec(0), trc_spec(0), tok_spec(1), tr_spec(1), tok_spec(2), tr_spec(2), trc_spec(2)],
        out_shape=[tok_shape, trc_shape, tok_shape, tr_shape, tok_shape, tr_shape, trc_shape],
        scratch_shapes=[pltpu.VMEM((tm, d), BF16)],
        compiler_params=_cparams(("parallel", "parallel", "arbitrary")),
        name="inproj_qkv",
    )(x2, g_pre.reshape(1, d), w_qkv)


def _ssd_kernel(xbc_ref, z_ref, fdt_ref, hist_ref, init_ref, cw_ref, cb_ref, bias_ref, alog_ref,
                dsk_ref, gssm_ref, e_ref, perm_ref,
                y_ref, logft_ref, aug_ref, fin_ref,
                xext_ref, st_ref, carry_ref, *, L, H, P, G, N, HF, CW):
    c = pl.program_id(1)
    nc = pl.num_programs(1)
    DS = H * P
    DSG = DS // G
    HG = H // G

    @pl.when(c == 0)
    def _():
        xext_ref[0:SUBLANES, :] = hist_ref[0]
        st_ref[...] = init_ref[0].T
        carry_ref[...] = jnp.zeros_like(carry_ref)

    xext_ref[SUBLANES:SUBLANES + L, :] = xbc_ref[...]
    cw = cw_ref[...]
    conv = cb_ref[...]
    for kk in range(CW):
        off = SUBLANES - (CW - 1) + kk
        conv = conv + xext_ref[off:off + L, :] * cw[kk:kk + 1, :]
    xext_ref[0:SUBLANES, :] = xext_ref[L:L + SUBLANES, :]
    xc = _silu(conv)
    xs = xc[:, :DS]
    bm = xc[:, DS:DS + G * N]
    cm = xc[:, DS + G * N:]

    lane = lax.broadcasted_iota(jnp.int32, (1, LANES), 1)
    is_f = lane < HF
    is_dt = jnp.logical_and(lane >= HF, lane < HF + H)
    sp, lsg = _softplus_terms(fdt_ref[...] + bias_ref[...])
    dt = jnp.where(is_dt, sp, 0.0)
    da = dt * (-jnp.exp(alog_ref[...]))
    lf = jnp.where(is_f, lsg, 0.0)
    rowi = lax.broadcasted_iota(jnp.int32, (L, 1), 0)
    comb = da + lf + jnp.where(rowi == 0, carry_ref[...], 0.0)

    r_i = lax.broadcasted_iota(jnp.int32, (L, L), 0)
    c_i = lax.broadcasted_iota(jnp.int32, (L, L), 1)
    causal = r_i >= c_i
    tril = jnp.where(causal, 1.0, 0.0).astype(BF16)
    hi, mid, lo = _split3(comb)
    cum3 = _dot(tril, jnp.concatenate([hi, mid, lo], axis=1))
    cum = cum3[:, :LANES] + cum3[:, LANES:2 * LANES] + cum3[:, 2 * LANES:]
    carry_ref[...] = jnp.where(is_f, cum[L - 1:L, :], 0.0)
    cum_t = cum.T
    logft_ref[0] = lsg.T[0:HF, :]
    bh, bmid, bl = _split3(jnp.where(is_f, cum * (-LOG2E), 0.0))
    aug_ref[...] = _dot(jnp.concatenate([bh, bmid, bl], axis=1), perm_ref[...]).astype(BF16)

    a_last = cum[L - 1:L, :]
    qq = jnp.concatenate([dt, jnp.where(is_dt, jnp.exp(cum), 0.0),
                          jnp.where(is_dt, jnp.exp(a_last - cum), 0.0)], axis=0)
    qhi, qlo = _split2(qq)
    ex = _dot(jnp.concatenate([qhi, qlo], axis=0), e_ref[...])
    ex = ex[:3 * L] + ex[3 * L:]
    dt_e = ex[0:L]
    ea_e = ex[L:2 * L]
    dte_e = ex[2 * L:3 * L]

    xdt = xs * dt_e
    xdte_b = (xdt * dte_e).astype(BF16)
    lane_p = lax.broadcasted_iota(jnp.int32, (1, 2 * P), 1)
    first_head = lane_p < P

    y_groups = []
    for g in range(G):
        gs = slice(g * DSG, (g + 1) * DSG)
        bg_t = bm[:, g * N:(g + 1) * N].T.astype(BF16)
        cg = cm[:, g * N:(g + 1) * N].astype(BF16)
        cb = _dot(cg, bg_t)
        s_t = st_ref[:, gs]
        y_off = _dot(cg, s_t.astype(BF16)) * ea_e[:, gs]
        y_pairs = []
        for jp in range(HG // 2):
            e0 = g * HG + 2 * jp
            ms = []
            for e in (e0, e0 + 1):
                col = HF + e
                seg = cum[:, col:col + 1] - cum_t[col:col + 1, :]
                ms.append((cb * jnp.exp(jnp.where(causal, seg, -jnp.inf))).astype(BF16))
            lhs = jnp.concatenate(ms, axis=1)
            xp = xdt[:, e0 * P:(e0 + 2) * P]
            rhs = jnp.concatenate([jnp.where(first_head, xp, 0.0).astype(BF16),
                                   jnp.where(first_head, 0.0, xp).astype(BF16)], axis=0)
            y_pairs.append(_dot(lhs, rhs))
        y_groups.append(jnp.concatenate(y_pairs, axis=1) + y_off)
        st_ref[:, gs] = s_t * ea_e[L - 1:L, gs] + _dot(bg_t, xdte_b[:, gs])
    y = jnp.concatenate(y_groups, axis=1) + dsk_ref[...] * xs

    yg = y * _silu(z_ref[...])
    inv = lax.rsqrt(jnp.mean(yg * yg, axis=-1, keepdims=True) + RMS_EPS)
    y_ref[...] = ((yg * inv) * gssm_ref[...]).astype(BF16)

    @pl.when(c == nc - 1)
    def _():
        fin_ref[0] = st_ref[...].T


def _ssd_prompt(xbc, z, fdt, hist8, init, conv_w, conv_b, bias128, alog128, dsk_e, g_ssm, e_mat, perm,
                *, B, T, L, H, P, G, N, HF):
    DS = H * P
    cd = xbc.shape[1]
    CW = conv_w.shape[0]
    nc = T // L
    assert T % L == 0 and L == LANES and 2 * P == LANES and H % (2 * G) == 0
    assert HF + H <= LANES and 3 * HF <= LANES
    kern = functools.partial(_ssd_kernel, L=L, H=H, P=P, G=G, N=N, HF=HF, CW=CW)
    tok = lambda b, c: (b * nc + c, 0)
    const2 = lambda b, c: (0, 0)
    return pl.pallas_call(
        kern,
        grid=(B, nc),
        in_specs=[pl.BlockSpec((L, cd), tok),
                  pl.BlockSpec((L, DS), tok),
                  pl.BlockSpec((L, LANES), tok),
                  pl.BlockSpec((1, SUBLANES, cd), lambda b, c: (b, 0, 0)),
                  pl.BlockSpec((1, DS, N), lambda b, c: (b, 0, 0)),
                  pl.BlockSpec((CW, cd), const2),
                  pl.BlockSpec((1, cd), const2),
                  pl.BlockSpec((1, LANES), const2),
                  pl.BlockSpec((1, LANES), const2),
                  pl.BlockSpec((1, DS), const2),
                  pl.BlockSpec((1, DS), const2),
                  pl.BlockSpec((LANES, DS), const2),
                  pl.BlockSpec((3 * LANES, LANES), const2)],
        out_specs=[pl.BlockSpec((L, DS), tok),
                   pl.BlockSpec((1, HF, L), lambda b, c: (b, 0, c)),
                   pl.BlockSpec((L, LANES), tok),
                   pl.BlockSpec((1, DS, N), lambda b, c: (b, 0, 0))],
        out_shape=[jax.ShapeDtypeStruct((B * T, DS), BF16),
                   jax.ShapeDtypeStruct((B, HF, T), F32),
                   jax.ShapeDtypeStruct((B * T, LANES), BF16),
                   jax.ShapeDtypeStruct((B, DS, N), F32)],
        scratch_shapes=[pltpu.VMEM((L + SUBLANES, cd), F32),
                        pltpu.VMEM((N, DS), F32),
                        pltpu.VMEM((1, LANES), F32)],
        compiler_params=_cparams(("arbitrary", "arbitrary")),
        name="ssd_prompt",
    )(xbc, z, fdt, hist8, init, conv_w, conv_b, bias128, alog128, dsk_e, g_ssm, e_mat, perm)


def _attn_kernel(qt_ref, k_ref, aug_ref, vt_ref, fg_ref, o_ref,
                 sa_ref, sb_ref, m_ref, l_ref, acc_ref, qa_ref, *, t, dh, HF):
    hp = pl.program_id(1)
    qi = pl.program_id(2)
    rowp = lax.broadcasted_iota(jnp.int32, (2 * dh, 1), 0)
    first = rowp < dh
    qf = qt_ref[0, 0].astype(F32)
    for h in range(2):
        head = 2 * hp + h
        hit = jnp.logical_or(rowp == head, jnp.logical_or(rowp == head + HF, rowp == head + 2 * HF))
        sel = jnp.broadcast_to(jnp.where(hit, 1.0, 0.0), (2 * dh, t))
        qa = jnp.where(first, qf, 0.0) if h == 0 else jnp.where(first, 0.0, qf)
        qa_ref[h] = jnp.concatenate([qa, sel], axis=0).astype(BF16)
    m_ref[...] = jnp.full(m_ref.shape, -jnp.inf, F32)
    l_ref[...] = jnp.zeros(l_ref.shape, F32)
    acc_ref[...] = jnp.zeros(acc_ref.shape, F32)

    def scores(ki, s_ref):
        off = pl.multiple_of(ki * t, t)
        ka = jnp.concatenate([k_ref[pl.ds(off, t), :], aug_ref[pl.ds(off, t), :]], axis=1)
        for h in range(2):
            s_ref[h] = _dot(ka, qa_ref[h])

    def absorb(ki, s_ref, mask):
        vt = vt_ref[0, ki]
        for h in range(2):
            s = s_ref[h]
            if mask is not None:
                s = jnp.where(mask, s, -jnp.inf)
            m = m_ref[h]
            m_new = jnp.maximum(m, jnp.max(s, axis=0, keepdims=True))
            alpha = jnp.exp2(m - m_new)
            p = jnp.exp2(s - m_new)
            l_ref[h] = alpha * l_ref[h] + jnp.sum(p, axis=0, keepdims=True)
            m_ref[h] = m_new
            acc_ref[h] = alpha * acc_ref[h] + _dot(vt, p.astype(BF16))

    scores(0, sa_ref)

    def pair(pp, carry):
        ki = 2 * pp
        scores(ki + 1, sb_ref)
        absorb(ki, sa_ref, None)
        scores(ki + 2, sa_ref)
        absorb(ki + 1, sb_ref, None)
        return carry

    lax.fori_loop(0, qi // 2, pair, 0)
    r_i = lax.broadcasted_iota(jnp.int32, (t, t), 0)
    c_i = lax.broadcasted_iota(jnp.int32, (t, t), 1)
    diag = r_i <= c_i
    odd = lax.rem(qi, 2) == 1

    @pl.when(odd)
    def _():
        scores(qi, sb_ref)
        absorb(qi - 1, sa_ref, None)
        absorb(qi, sb_ref, diag)

    @pl.when(jnp.logical_not(odd))
    def _():
        absorb(qi, sa_ref, diag)

    ot = jnp.where(first, acc_ref[0] / l_ref[0], acc_ref[1] / l_ref[1])
    o_ref[...] = (ot.T * _silu(fg_ref[...])).astype(BF16)


def _attn_prompt(qt4, kb, aug, vt4, fg, *, B, T, HF, dh):
    t = qt4.shape[3]
    nq = T // t
    assert T % t == 0 and 2 * dh == LANES and HF % 2 == 0 and qt4.shape == (B, nq, HF * dh, t)
    npair = HF // 2
    return pl.pallas_call(
        functools.partial(_attn_kernel, t=t, dh=dh, HF=HF),
        grid=(B, npair, nq),
        in_specs=[pl.BlockSpec((1, 1, LANES, t), lambda b, hp, qi: (b, qi, hp, 0)),
                  pl.BlockSpec((T, LANES), lambda b, hp, qi: (b, hp)),
                  pl.BlockSpec((T, LANES), lambda b, hp, qi: (b, 0)),
                  pl.BlockSpec((1, nq, LANES, t), lambda b, hp, qi: (b, 0, hp, 0)),
                  pl.BlockSpec((t, LANES), lambda b, hp, qi: (b * nq + qi, hp))],
        out_specs=pl.BlockSpec((t, LANES), lambda b, hp, qi: (b * nq + qi, hp)),
        out_shape=jax.ShapeDtypeStruct((B * T, HF * dh), BF16),
        scratch_shapes=[pltpu.VMEM((2, t, t), F32), pltpu.VMEM((2, t, t), F32),
                        pltpu.VMEM((2, 1, t), F32), pltpu.VMEM((2, 1, t), F32),
                        pltpu.VMEM((2, LANES, t), F32), pltpu.VMEM((2, 2 * LANES, t), BF16)],
        compiler_params=_cparams(("parallel", "parallel", "arbitrary")),
        name="fox_prompt",
    )(qt4, kb, aug, vt4, fg)


def _sample_pre_kernel(xbc_ref, sconv_ref, fdt_ref, cw_ref, cb_ref, bias_ref, alog_ref, e_ref,
                       logf_ref, xs_ref, xdt_t_ref, ed_t_ref, bs_ref, cs_ref, nconv_ref,
                       *, H, P, G, N, HF, CW):
    DS = H * P
    cd = xbc_ref.shape[1]
    x = xbc_ref[...]
    rows = [sconv_ref[:, kk * cd:(kk + 1) * cd] for kk in range(CW - 1)] + [x]
    cw = cw_ref[...]
    conv = cb_ref[...]
    for kk in range(CW):
        conv = conv + rows[kk] * cw[kk:kk + 1, :]
    nconv_ref[...] = jnp.concatenate(rows[1:], axis=1)
    xc = _silu(conv)
    xs = xc[:, :DS]
    xs_ref[...] = xs
    bs_ref[...] = xc[:, DS:DS + G * N]
    cs_ref[...] = xc[:, DS + G * N:]

    lane = lax.broadcasted_iota(jnp.int32, (1, LANES), 1)
    is_dt = jnp.logical_and(lane >= HF, lane < HF + H)
    sp, lsg = _softplus_terms(fdt_ref[...] + bias_ref[...])
    logf_ref[...] = lsg[:, :HF]
    dt = jnp.where(is_dt, sp, 0.0)
    ed = jnp.where(is_dt, jnp.exp(dt * (-jnp.exp(alog_ref[...]))), 0.0)
    s = dt.shape[0]
    hi, mid, lo = _split3(jnp.concatenate([dt, ed], axis=0))
    ex = _dot(jnp.concatenate([hi, mid, lo], axis=0), e_ref[...])
    ex = ex[:2 * s] + ex[2 * s:4 * s] + ex[4 * s:]
    xdt_t_ref[...] = (xs * ex[:s]).T
    ed_t_ref[...] = ex[s:].T


def _sample_pre(xbc, sconv, fdt, conv_w, conv_b, bias128, alog128, e_mat, *, H, P, G, N, HF):
    s, cd = xbc.shape
    DS = H * P
    CW = conv_w.shape[0]
    kern = functools.partial(_sample_pre_kernel, H=H, P=P, G=G, N=N, HF=HF, CW=CW)
    return pl.pallas_call(
        kern,
        out_shape=[jax.ShapeDtypeStruct((s, HF), F32),
                   jax.ShapeDtypeStruct((s, DS), F32),
                   jax.ShapeDtypeStruct((DS, s), F32),
                   jax.ShapeDtypeStruct((DS, s), F32),
                   jax.ShapeDtypeStruct((s, G * N), F32),
                   jax.ShapeDtypeStruct((s, G * N), F32),
                   jax.ShapeDtypeStruct((s, (CW - 1) * cd), F32)],
        compiler_params=pltpu.CompilerParams(vmem_limit_bytes=VMEM_LIMIT),
        name="sample_pre",
    )(xbc, sconv, fdt, conv_w, conv_b, bias128, alog128, e_mat)


def _sample_state_kernel(s0_ref, xdt_t_ref, ed_t_ref, bs_ref, cs_ref, xs_ref, z_ref, dsk_ref, gssm_ref,
                         new_ref, y_ref, *, H, P, G, N):
    si = pl.program_id(0)
    DS = H * P
    DSG = DS // G
    ns = xdt_t_ref.shape[1]
    lane = lax.broadcasted_iota(jnp.int32, (1, ns), 1)
    pick = lane == si
    xcol = jnp.sum(jnp.where(pick, xdt_t_ref[...], 0.0), axis=1, keepdims=True)
    ecol = jnp.sum(jnp.where(pick, ed_t_ref[...], 0.0), axis=1, keepdims=True)
    bs = bs_ref[0]
    cs = cs_ref[0]
    ys = []
    for g in range(G):
        gs = slice(g * DSG, (g + 1) * DSG)
        new_g = s0_ref[0, gs, :] * ecol[gs] + xcol[gs] * bs[:, g * N:(g + 1) * N]
        new_ref[0, gs, :] = new_g
        chi, clo = _split2(cs[:, g * N:(g + 1) * N])
        cst = jnp.concatenate([chi, clo, jnp.zeros((SUBLANES - 2, N), BF16)], axis=0)
        nhi, nlo = _split2(new_g)
        r = _dot_nt(cst, nhi) + _dot_nt(cst, nlo)
        ys.append(r[0:1] + r[1:2])
    y = jnp.concatenate(ys, axis=1) + dsk_ref[...] * xs_ref[0]
    yg = y * _silu(z_ref[0])
    inv = lax.rsqrt(jnp.mean(yg * yg, axis=-1, keepdims=True) + RMS_EPS)
    y_ref[0] = ((yg * inv) * gssm_ref[...]).astype(BF16)


def _sample_state(s0, xdt_t, ed_t, bs, cs, xs, z, dsk_e, g_ssm, *, H, P, G, N):
    ns, DS, _ = s0.shape
    row3 = lambda a: a.reshape(ns, 1, a.shape[-1])
    per = lambda w: pl.BlockSpec((1, 1, w), lambda s: (s, 0, 0))
    const2 = lambda s: (0, 0)
    new, y = pl.pallas_call(
        functools.partial(_sample_state_kernel, H=H, P=P, G=G, N=N),
        grid=(ns,),
        in_specs=[pl.BlockSpec((1, DS, N), lambda s: (s, 0, 0)),
                  pl.BlockSpec((DS, ns), const2),
                  pl.BlockSpec((DS, ns), const2),
                  per(G * N), per(G * N), per(DS), per(DS),
                  pl.BlockSpec((1, DS), const2),
                  pl.BlockSpec((1, DS), const2)],
        out_specs=[pl.BlockSpec((1, DS, N), lambda s: (s, 0, 0)), per(DS)],
        out_shape=[jax.ShapeDtypeStruct((ns, DS, N), F32),
                   jax.ShapeDtypeStruct((ns, 1, DS), BF16)],
        compiler_params=_cparams(("parallel",)),
        name="sample_state",
    )(s0, xdt_t, ed_t, row3(bs), row3(cs), row3(xs), row3(z), dsk_e, g_ssm)
    return new, y.reshape(ns, DS)


_PAGES_PER_STEP = 16


def _paged_kernel(pt_ref, q_ref, kn_ref, vn_ref, lfn_ref, fg_ref, ind_ref, *rest, PP, HF, DF):
    k_refs = rest[0:PP]
    v_refs = rest[PP:2 * PP]
    lf_refs = rest[2 * PP:3 * PP]
    o_ref = rest[3 * PP]
    m_ref, l_ref, r_ref, acc_ref, qrow_ref = rest[3 * PP + 1:]
    j = pl.program_id(1)
    nj = pl.num_programs(1)
    ps = k_refs[0].shape[2]
    ind = ind_ref[...]

    @pl.when(j == 0)
    def _():
        qrow = ind * q_ref[0].astype(F32)
        qrow_ref[...] = qrow.astype(BF16)
        m_ref[...] = jnp.sum(qrow * kn_ref[0].astype(F32), axis=1, keepdims=True)
        l_ref[...] = jnp.ones_like(l_ref)
        r_ref[...] = lfn_ref[0]
        acc_ref[...] = ind * vn_ref[0].astype(F32)

    kcat = jnp.concatenate([k_refs[r][0].astype(BF16) for r in range(PP)], axis=1)
    s = _dot(qrow_ref[...], kcat)

    lfs = [lf_refs[r][0] for r in range(PP)]
    r_i = lax.broadcasted_iota(jnp.int32, (ps, ps), 0)
    c_i = lax.broadcasted_iota(jnp.int32, (ps, ps), 1)
    later = jnp.where(r_i > c_i, 1.0, 0.0).astype(BF16)
    hi, mid, lo = _split3(jnp.concatenate(lfs, axis=0))
    n = PP * HF
    suf3 = _dot(jnp.concatenate([hi, mid, lo], axis=0), later)
    suf = suf3[:n] + suf3[n:2 * n] + suf3[2 * n:]
    rcur = r_ref[...]
    biases = []
    for r in range(PP):
        biases.append(suf[r * HF:(r + 1) * HF] + rcur)
        rcur = rcur + jnp.sum(lfs[r], axis=1, keepdims=True)
    r_ref[...] = rcur
    s = s + jnp.concatenate(biases, axis=1) * LOG2E

    m_old = m_ref[...]
    m_new = jnp.maximum(m_old, jnp.max(s, axis=1, keepdims=True))
    alpha = jnp.exp2(m_old - m_new)
    p = jnp.exp2(s - m_new)
    l_ref[...] = alpha * l_ref[...] + jnp.sum(p, axis=1, keepdims=True)
    m_ref[...] = m_new
    vcat = jnp.concatenate([v_refs[r][0].astype(BF16) for r in range(PP)], axis=1)
    acc_ref[...] = alpha * acc_ref[...] + _dot_nt(p.astype(BF16), vcat)

    @pl.when(j == nj - 1)
    def _():
        o = jnp.sum(acc_ref[...] * ind / l_ref[...], axis=0, keepdims=True)
        o_ref[0] = (o * _silu(fg_ref[0])).astype(BF16)


def _paged_attn(page_idx, q, kn, vn, lfn, fg, ind, ckt, cvt, clft, *, HF, DF):
    ns, npages = page_idx.shape
    ps = ckt.shape[2]
    PP = min(_PAGES_PER_STEP, npages)
    assert npages % PP == 0
    nj = npages // PP
    row3 = lambda a: a.reshape(ns, 1, a.shape[-1])
    per = lambda w: pl.BlockSpec((1, 1, w), lambda s, j, pt: (s, 0, 0))

    def page_spec(r, rows):
        return pl.BlockSpec((1, rows, ps), lambda s, j, pt: (pt[s, npages - 1 - (j * PP + r)], 0, 0))

    grid_spec = pltpu.PrefetchScalarGridSpec(
        num_scalar_prefetch=1,
        grid=(ns, nj),
        in_specs=[per(DF), per(DF), per(DF),
                  pl.BlockSpec((1, HF, 1), lambda s, j, pt: (s, 0, 0)),
                  per(DF),
                  pl.BlockSpec((HF, DF), lambda s, j, pt: (0, 0))]
                 + [page_spec(r, DF) for r in range(PP)]
                 + [page_spec(r, DF) for r in range(PP)]
                 + [page_spec(r, HF) for r in range(PP)],
        out_specs=per(DF),
        scratch_shapes=[pltpu.VMEM((HF, 1), F32), pltpu.VMEM((HF, 1), F32), pltpu.VMEM((HF, 1), F32),
                        pltpu.VMEM((HF, DF), F32), pltpu.VMEM((HF, DF), BF16)],
    )
    out = pl.pallas_call(
        functools.partial(_paged_kernel, PP=PP, HF=HF, DF=DF),
        grid_spec=grid_spec,
        out_shape=jax.ShapeDtypeStruct((ns, 1, DF), BF16),
        compiler_params=_cparams(("parallel", "arbitrary")),
        name="fox_paged",
    )(page_idx, row3(q), row3(kn), row3(vn), lfn.reshape(ns, HF, 1), row3(fg), ind,
      *([ckt] * PP), *([cvt] * PP), *([clft] * PP))
    return out.reshape(ns, DF)


def _final_kernel(ys_ref, yf_ref, ga_ref, gb_ref, x_ref, pe_ref,
                  wso_ref, wfo_ref, wo_ref, wpg_ref, wple_ref, gpost_ref, gple_ref, o_ref):
    a = _dot(ys_ref[...], wso_ref[...])
    b = _dot(yf_ref[...], wfo_ref[...])
    m = jax.nn.sigmoid(ga_ref[...]) * a + jax.nn.sigmoid(gb_ref[...]) * b
    o = _dot(m.astype(BF16), wo_ref[...])
    inv = lax.rsqrt(jnp.mean(o * o, axis=-1, keepdims=True) + RMS_EPS)
    x1 = x_ref[...] + (o * inv) * gpost_ref[...]
    inv1 = lax.rsqrt(jnp.mean(x1 * x1, axis=-1, keepdims=True) + RMS_EPS)
    hg = ((x1 * inv1) * gple_ref[...]).astype(BF16)
    gate = jax.nn.sigmoid(_dot(hg, wpg_ref[...]))
    pe = _dot(pe_ref[...].astype(BF16), wple_ref[...])
    o_ref[...] = x1 + gate * pe


def _final(ys, yf, ga, gb, x2, pemb, wso, wfo, wo, wpg, wple, g_post, g_ple):
    m, d = x2.shape
    tm = min(512, m)
    assert m % tm == 0
    tok = lambda w: pl.BlockSpec((tm, w), lambda i: (i, 0))
    full = lambda a: pl.BlockSpec(a.shape, lambda i: (0, 0))
    gp = g_post.reshape(1, d)
    gl = g_ple.reshape(1, d)
    return pl.pallas_call(
        _final_kernel,
        grid=(m // tm,),
        in_specs=[tok(ys.shape[1]), tok(yf.shape[1]), tok(d), tok(d), tok(d), tok(pemb.shape[1]),
                  full(wso), full(wfo), full(wo), full(wpg), full(wple), full(gp), full(gl)],
        out_specs=tok(d),
        out_shape=jax.ShapeDtypeStruct((m, d), F32),
        compiler_params=_cparams(("parallel",)),
        name="merge_out",
    )(ys, yf, ga, gb, x2, pemb, wso, wfo, wo, wpg, wple, gp, gl)


def kernel(x_prompt, x_sample, cache_k, cache_v, cache_logf, state_ssm, state_conv, page_table, p_prompt, p_sample, g_pre, w_in, conv_w, conv_b, dt_bias, a_log, d_skip, g_ssm, b_fgate, w_ssm_out, w_fox_out, w_out, g_post, w_ple, w_ple_gate, g_ple):
    B, T, D = x_prompt.shape
    S, TS, _ = x_sample.shape
    assert TS == 1
    depth = w_in.shape[0]
    _, n_phys, ps, HF, dh = cache_k.shape
    _, _, H, P, N = state_ssm.shape
    CW, cd = conv_w.shape[1], conv_w.shape[2]
    DS = H * P
    DF = HF * dh
    G = (cd - DS) // (2 * N)
    L = LANES
    q_scale = float(dh) ** -0.5 * LOG2E
    pad_l = LANES - HF - H

    ch_s = jnp.arange(DS) // P
    e_ssm = (jnp.arange(LANES)[:, None] == (HF + ch_s)[None, :]).astype(BF16)
    ind_fox = (jnp.arange(HF)[:, None] == (jnp.arange(DF) // dh)[None, :]).astype(F32)
    src = jnp.arange(3 * LANES)
    dst = (src // LANES) * HF + (src % LANES)
    perm = jnp.logical_and((src % LANES < HF)[:, None], dst[:, None] == jnp.arange(LANES)[None, :]).astype(BF16)

    ckt = jnp.transpose(cache_k, (0, 1, 3, 4, 2)).reshape(depth * n_phys, DF, ps)
    cvt = jnp.transpose(cache_v, (0, 1, 3, 4, 2)).reshape(depth * n_phys, DF, ps)
    clft = jnp.transpose(cache_logf, (0, 1, 3, 2)).reshape(depth * n_phys, HF, ps)

    xp = x_prompt.reshape(B * T, D)
    xs_tok = x_sample.reshape(S, D)
    outs_p = ([], [], [], [], [])
    outs_s = ([], [], [], [], [])
    for i in range(depth):
        z_w, xbc_w, dt_w, q_w, k_w, v_w, f_w, fg_w, ga_w, gb_w = jnp.split(
            w_in[i], [DS, DS + cd, DS + cd + H, DS + cd + H + DF, DS + cd + H + 2 * DF,
                      DS + cd + H + 3 * DF, DS + cd + H + 3 * DF + HF, DS + cd + H + 4 * DF + HF,
                      DS + cd + H + 4 * DF + HF + D], axis=1)
        w_tok = jnp.concatenate([z_w, fg_w, ga_w, gb_w, xbc_w], axis=1).astype(BF16)
        tok_widths = (DS, DF, D, D, cd)
        w_qkv = jnp.concatenate([q_w, k_w, v_w], axis=1).astype(BF16)
        w_small = jnp.concatenate([f_w, dt_w, jnp.zeros((D, pad_l), F32)], axis=1).astype(BF16)
        bias128 = jnp.concatenate([b_fgate[i], dt_bias[i], jnp.zeros((pad_l,), F32)]).reshape(1, LANES)
        alog128 = jnp.concatenate([jnp.zeros((HF,), F32), a_log[i], jnp.zeros((pad_l,), F32)]).reshape(1, LANES)
        dsk_e = jnp.repeat(d_skip[i], P).reshape(1, DS)
        gssm = g_ssm[i].reshape(1, DS)
        cw_i = conv_w[i]
        cb_i = conv_b[i].reshape(1, cd)
        wso = w_ssm_out[i].astype(BF16)
        wfo = w_fox_out[i].astype(BF16)
        wo = w_out[i].astype(BF16)
        wpg = w_ple_gate[i].astype(BF16)
        wple = w_ple[i].astype(BF16)
        proj_tok = functools.partial(_inproj_tok, g_pre=g_pre[i], w_tok=w_tok, w_small=w_small, widths=tok_widths)
        proj_qkv = functools.partial(_inproj_qkv, g_pre=g_pre[i], w_qkv=w_qkv, d_fox=DF, q_scale=q_scale)
        fin = functools.partial(_final, wso=wso, wfo=wfo, wo=wo, wpg=wpg, wple=wple,
                                g_post=g_post[i], g_ple=g_ple[i])

        z, fg, ga, gb, xbc, fdt = proj_tok(xp)
        _, qt4, kb, kt, _, vt, vt4 = proj_qkv(xp, B=B, T=T)
        hist8 = jnp.zeros((B, SUBLANES, cd), F32)
        init = jnp.zeros((B, DS, N), F32)
        y_ssm, logft, aug, ssm_fin = _ssd_prompt(xbc, z, fdt, hist8, init, cw_i, cb_i, bias128, alog128,
                                                 dsk_e, gssm, e_ssm, perm, B=B, T=T, L=L, H=H, P=P, G=G, N=N, HF=HF)
        y_fox = _attn_prompt(qt4, kb, aug, vt4, fg, B=B, T=T, HF=HF, dh=dh)
        xp = fin(y_ssm, y_fox, ga, gb, xp, p_prompt[i].reshape(B * T, -1))
        xbc3 = xbc.reshape(B, T, cd)
        hist_p = jnp.zeros((B, CW - 1, cd), F32)
        conv_p = jnp.concatenate([hist_p, xbc3], axis=1)[:, T:] if T < CW - 1 else xbc3[:, T - (CW - 1):]
        k_out = jnp.transpose(kt.reshape(B, HF, dh, T), (0, 3, 1, 2))
        v_out = jnp.transpose(vt.reshape(B, HF, dh, T), (0, 3, 1, 2))
        for lst, arr in zip(outs_p, (k_out, v_out, jnp.transpose(logft, (0, 2, 1)),
                                     ssm_fin.reshape(B, H, P, N), conv_p)):
            lst.append(arr)

        z, fg, ga, gb, xbc, fdt = proj_tok(xs_tok)
        qb, _, kb, kt, vb, vt, _ = proj_qkv(xs_tok, B=1, T=S)
        logf_s, xs_c, xdt_t, ed_t, bs, cs, nconv = _sample_pre(
            xbc, state_conv[i].reshape(S, (CW - 1) * cd), fdt, cw_i, cb_i, bias128, alog128, e_ssm,
            H=H, P=P, G=G, N=N, HF=HF)
        ssm_new, y_ssm = _sample_state(state_ssm[i].reshape(S, DS, N), xdt_t, ed_t, bs, cs, xs_c, z,
                                       dsk_e, gssm, H=H, P=P, G=G, N=N)
        y_fox = _paged_attn(page_table + i * n_phys, qb, kb, vb, logf_s, fg, ind_fox, ckt, cvt, clft, HF=HF, DF=DF)
        xs_tok = fin(y_ssm, y_fox, ga, gb, xs_tok, p_sample[i].reshape(S, -1))
        k_out = jnp.transpose(kt.reshape(HF, dh, S), (2, 0, 1)).reshape(S, 1, HF, dh)
        v_out = jnp.transpose(vt.reshape(HF, dh, S), (2, 0, 1)).reshape(S, 1, HF, dh)
        for lst, arr in zip(outs_s, (k_out, v_out, logf_s.reshape(S, 1, HF),
                                     ssm_new.reshape(S, H, P, N), nconv.reshape(S, CW - 1, cd))):
            lst.append(arr)

    k_p, v_p, lf_p, ssm_p, conv_p = [jnp.stack(a) for a in outs_p]
    k_s, v_s, lf_s, ssm_s, conv_s = [jnp.stack(a) for a in outs_s]
    return (xp.reshape(B, T, D), xs_tok.reshape(S, 1, D), k_p, v_p, lf_p, ssm_p, conv_p,
            k_s, v_s, lf_s, ssm_s, conv_s)
```

```python
import functools
import math

import jax
import jax.numpy as jnp
from jax import lax
from jax.experimental import pallas as pl
from jax.experimental.pallas import tpu as pltpu

F32 = jnp.float32
BF16 = jnp.bfloat16
RMS_EPS = 1e-6
LOG2E = math.log2(math.e)
LANES = 128
SUBLANES = 8
VMEM_LIMIT = 56 * 1024 * 1024


def _cparams(sem):
    return pltpu.CompilerParams(dimension_semantics=sem, vmem_limit_bytes=VMEM_LIMIT)


def _silu(x):
    return x * jax.nn.sigmoid(x)


def _softplus_terms(x):
    t = jnp.log1p(jnp.exp(-jnp.abs(x)))
    return jnp.maximum(x, 0.0) + t, jnp.minimum(x, 0.0) - t


def _split2(x):
    hi = x.astype(BF16)
    lo = (x - hi.astype(F32)).astype(BF16)
    return hi, lo


def _split3(x):
    hi = x.astype(BF16)
    r = x - hi.astype(F32)
    mid = r.astype(BF16)
    lo = (r - mid.astype(F32)).astype(BF16)
    return hi, mid, lo


def _dot(a, b):
    return jnp.dot(a, b, preferred_element_type=F32)


def _dot_nt(a, b):
    return lax.dot_general(a, b, (((1,), (1,)), ((), ())), preferred_element_type=F32)


def _prenorm(x_ref, g_ref):
    x = x_ref[...]
    inv = lax.rsqrt(jnp.mean(x * x, axis=-1, keepdims=True) + RMS_EPS)
    return ((x * inv) * g_ref[...]).astype(BF16)


_IN_TN = 512
_IN_TM = 1024
_ATT_T = 512


def _inproj_tok_kernel(x_ref, g_ref, w_ref, ws_ref, z_ref, fg_ref, ga_ref, gb_ref, xbc_ref, fdt_ref,
                       h_ref, *, seg_starts):
    j = pl.program_id(1)

    @pl.when(j == 0)
    def _():
        hb = _prenorm(x_ref, g_ref)
        h_ref[...] = hb
        fdt_ref[...] = _dot(hb, ws_ref[...])

    outs = (z_ref, fg_ref, ga_ref, gb_ref, xbc_ref)
    for si, o_ref in enumerate(outs):
        @pl.when(jnp.logical_and(j >= seg_starts[si], j < seg_starts[si + 1]))
        def _(o_ref=o_ref):
            o_ref[...] = _dot(h_ref[...], w_ref[...]).astype(o_ref.dtype)


def _inproj_tok(x2, g_pre, w_tok, w_small, *, widths, dtypes):
    m, d = x2.shape
    tn = _IN_TN
    tm = min(_IN_TM, m)
    assert m % tm == 0 and all(w % tn == 0 for w in widths)
    starts = [0]
    for w in widths:
        starts.append(starts[-1] + w // tn)
    nj = starts[-1]
    assert w_tok.shape == (d, nj * tn)

    def seg_spec(si):
        lo, nb = starts[si], widths[si] // tn
        return pl.BlockSpec((tm, tn), lambda i, j: (i, jnp.clip(j - lo, 0, nb - 1)))

    return pl.pallas_call(
        functools.partial(_inproj_tok_kernel, seg_starts=tuple(starts)),
        grid=(m // tm, nj),
        in_specs=[pl.BlockSpec((tm, d), lambda i, j: (i, 0)),
                  pl.BlockSpec((1, d), lambda i, j: (0, 0)),
                  pl.BlockSpec((d, tn), lambda i, j: (0, j)),
                  pl.BlockSpec((d, LANES), lambda i, j: (0, 0))],
        out_specs=[seg_spec(si) for si in range(len(widths))]
                  + [pl.BlockSpec((tm, LANES), lambda i, j: (i, 0))],
        out_shape=[jax.ShapeDtypeStruct((m, w), dt) for w, dt in zip(widths, dtypes)]
                  + [jax.ShapeDtypeStruct((m, LANES), F32)],
        scratch_shapes=[pltpu.VMEM((tm, d), BF16)],
        compiler_params=_cparams(("parallel", "arbitrary")),
        name="inproj_tok",
    )(x2, g_pre.reshape(1, d), w_tok, w_small)


def _inproj_qkv_kernel(x_ref, g_ref, w_ref, qb_ref, qt_ref, kb_ref, kt_ref, vb_ref, vt_ref, vtb_ref,
                       h_ref, *, nb, nch, tch, q_scale):
    j = pl.program_id(2)

    @pl.when(j == 0)
    def _():
        h_ref[...] = _prenorm(x_ref, g_ref)

    @pl.when(j < nb)
    def _():
        r = _dot(h_ref[...], w_ref[...]) * q_scale
        qb_ref[...] = r.astype(BF16)
        rt = r.T.astype(BF16)
        for c in range(nch):
            qt_ref[0, c] = rt[:, c * tch:(c + 1) * tch]

    @pl.when(jnp.logical_and(j >= nb, j < 2 * nb))
    def _():
        r = _dot(h_ref[...], w_ref[...])
        kb_ref[...] = r.astype(BF16)
        kt_ref[0] = r.T

    @pl.when(j >= 2 * nb)
    def _():
        r = _dot(h_ref[...], w_ref[...])
        vb_ref[...] = r.astype(BF16)
        rt = r.T
        vt_ref[0] = rt
        rtb = rt.astype(BF16)
        for c in range(nch):
            vtb_ref[0, c] = rtb[:, c * tch:(c + 1) * tch]


def _inproj_qkv(x2, g_pre, w_qkv, *, B, T, d_fox, q_scale):
    m, d = x2.shape
    tn = _IN_TN
    tch = min(_ATT_T, T)
    tm = min(_IN_TM, T)
    assert T % tm == 0 and tm % tch == 0 and d_fox % tn == 0 and m == B * T
    nt = T // tm
    nch = tm // tch
    nb = d_fox // tn
    assert w_qkv.shape == (d, 3 * d_fox)

    def tok_spec(seg):
        return pl.BlockSpec((tm, tn), lambda b, t, j: (b * nt + t, jnp.clip(j - seg * nb, 0, nb - 1)))

    def tr_spec(seg):
        return pl.BlockSpec((1, tn, tm), lambda b, t, j: (b, jnp.clip(j - seg * nb, 0, nb - 1), t))

    def trc_spec(seg):
        return pl.BlockSpec((1, nch, tn, tch), lambda b, t, j: (b, t, jnp.clip(j - seg * nb, 0, nb - 1), 0))

    tok_shape = jax.ShapeDtypeStruct((m, d_fox), BF16)
    tr_shape = jax.ShapeDtypeStruct((B, d_fox, T), F32)
    trc_shape = jax.ShapeDtypeStruct((B, T // tch, d_fox, tch), BF16)
    return pl.pallas_call(
        functools.partial(_inproj_qkv_kernel, nb=nb, nch=nch, tch=tch, q_scale=q_scale),
        grid=(B, nt, 3 * nb),
        in_specs=[pl.BlockSpec((tm, d), lambda b, t, j: (b * nt + t, 0)),
                  pl.BlockSpec((1, d), lambda b, t, j: (0, 0)),
                  pl.BlockSpec((d, tn), lambda b, t, j: (0, j))],
        out_specs=[tok_spec(0), trc_spec(0), tok_spec(1), tr_spec(1), tok_spec(2), tr_spec(2), trc_spec(2)],
        out_shape=[tok_shape, trc_shape, tok_shape, tr_shape, tok_shape, tr_shape, trc_shape],
        scratch_shapes=[pltpu.VMEM((tm, d), BF16)],
        compiler_params=_cparams(("parallel", "parallel", "arbitrary")),
        name="inproj_qkv",
    )(x2, g_pre.reshape(1, d), w_qkv)


def _ssd_kernel(xbc_ref, z_ref, fdt_ref, hist_ref, init_ref, cw_ref, cb_ref, bias_ref, alog_ref,
                dsk_ref, gssm_ref, e_ref, perm_ref,
                y_ref, logft_ref, aug_ref, fin_ref,
                xext_ref, st_ref, carry_ref, *, L, H, P, G, N, HF, CW):
    c = pl.program_id(1)
    nc = pl.num_programs(1)
    DS = H * P
    DSG = DS // G
    HG = H // G

    @pl.when(c == 0)
    def _():
        xext_ref[0:SUBLANES, :] = hist_ref[0]
        st_ref[...] = init_ref[0].T
        carry_ref[...] = jnp.zeros_like(carry_ref)

    xext_ref[SUBLANES:SUBLANES + L, :] = xbc_ref[...]
    cw = cw_ref[...]
    conv = cb_ref[...]
    for kk in range(CW):
        off = SUBLANES - (CW - 1) + kk
        conv = conv + xext_ref[off:off + L, :] * cw[kk:kk + 1, :]
    xext_ref[0:SUBLANES, :] = xext_ref[L:L + SUBLANES, :]
    xc = _silu(conv)
    xs = xc[:, :DS]
    bm = xc[:, DS:DS + G * N]
    cm = xc[:, DS + G * N:]

    lane = lax.broadcasted_iota(jnp.int32, (1, LANES), 1)
    is_f = lane < HF
    is_dt = jnp.logical_and(lane >= HF, lane < HF + H)
    sp, lsg = _softplus_terms(fdt_ref[...] + bias_ref[...])
    dt = jnp.where(is_dt, sp, 0.0)
    da = dt * (-jnp.exp(alog_ref[...]))
    lf = jnp.where(is_f, lsg, 0.0)
    rowi = lax.broadcasted_iota(jnp.int32, (L, 1), 0)
    comb = da + lf + jnp.where(rowi == 0, carry_ref[...], 0.0)

    r_i = lax.broadcasted_iota(jnp.int32, (L, L), 0)
    c_i = lax.broadcasted_iota(jnp.int32, (L, L), 1)
    causal = r_i >= c_i
    tril = jnp.where(causal, 1.0, 0.0).astype(BF16)
    hi, mid, lo = _split3(comb)
    cum3 = _dot(tril, jnp.concatenate([hi, mid, lo], axis=1))
    cum = cum3[:, :LANES] + cum3[:, LANES:2 * LANES] + cum3[:, 2 * LANES:]
    carry_ref[...] = jnp.where(is_f, cum[L - 1:L, :], 0.0)
    cum_t = cum.T
    logft_ref[0] = lsg.T[0:HF, :]
    bh, bmid, bl = _split3(jnp.where(is_f, cum * (-LOG2E), 0.0))
    aug_ref[...] = _dot(jnp.concatenate([bh, bmid, bl], axis=1), perm_ref[...]).astype(BF16)

    a_last = cum[L - 1:L, :]
    qq = jnp.concatenate([dt, jnp.where(is_dt, jnp.exp(cum), 0.0),
                          jnp.where(is_dt, jnp.exp(a_last - cum), 0.0)], axis=0)
    qhi, qlo = _split2(qq)
    ex = _dot(jnp.concatenate([qhi, qlo], axis=0), e_ref[...])
    ex = ex[:3 * L] + ex[3 * L:]
    dt_e = ex[0:L]
    ea_e = ex[L:2 * L]
    dte_e = ex[2 * L:3 * L]

    xdt = xs * dt_e
    xdte_b = (xdt * dte_e).astype(BF16)
    lane_p = lax.broadcasted_iota(jnp.int32, (1, 2 * P), 1)
    first_head = lane_p < P

    y_groups = []
    for g in range(G):
        gs = slice(g * DSG, (g + 1) * DSG)
        bg_t = bm[:, g * N:(g + 1) * N].T.astype(BF16)
        cg = cm[:, g * N:(g + 1) * N].astype(BF16)
        cb = _dot(cg, bg_t)
        s_t = st_ref[:, gs]
        y_off = _dot(cg, s_t.astype(BF16)) * ea_e[:, gs]
        y_pairs = []
        for jp in range(HG // 2):
            e0 = g * HG + 2 * jp
            ms = []
            for e in (e0, e0 + 1):
                col = HF + e
                seg = cum[:, col:col + 1] - cum_t[col:col + 1, :]
                ms.append((cb * jnp.exp(jnp.where(causal, seg, -jnp.inf))).astype(BF16))
            lhs = jnp.concatenate(ms, axis=1)
            xp = xdt[:, e0 * P:(e0 + 2) * P]
            rhs = jnp.concatenate([jnp.where(first_head, xp, 0.0).astype(BF16),
                                   jnp.where(first_head, 0.0, xp).astype(BF16)], axis=0)
            y_pairs.append(_dot(lhs, rhs))
        y_groups.append(jnp.concatenate(y_pairs, axis=1) + y_off)
        st_ref[:, gs] = s_t * ea_e[L - 1:L, gs] + _dot(bg_t, xdte_b[:, gs])
    y = jnp.concatenate(y_groups, axis=1) + dsk_ref[...] * xs

    yg = y * _silu(z_ref[...].astype(F32))
    inv = lax.rsqrt(jnp.mean(yg * yg, axis=-1, keepdims=True) + RMS_EPS)
    y_ref[...] = ((yg * inv) * gssm_ref[...]).astype(BF16)

    @pl.when(c == nc - 1)
    def _():
        fin_ref[0] = st_ref[...].T


def _ssd_prompt(xbc, z, fdt, hist8, init, conv_w, conv_b, bias128, alog128, dsk_e, g_ssm, e_mat, perm,
                *, B, T, L, H, P, G, N, HF):
    DS = H * P
    cd = xbc.shape[1]
    CW = conv_w.shape[0]
    nc = T // L
    assert T % L == 0 and L == LANES and 2 * P == LANES and H % (2 * G) == 0
    assert HF + H <= LANES and 3 * HF <= LANES
    kern = functools.partial(_ssd_kernel, L=L, H=H, P=P, G=G, N=N, HF=HF, CW=CW)
    tok = lambda b, c: (b * nc + c, 0)
    const2 = lambda b, c: (0, 0)
    return pl.pallas_call(
        kern,
        grid=(B, nc),
        in_specs=[pl.BlockSpec((L, cd), tok),
                  pl.BlockSpec((L, DS), tok),
                  pl.BlockSpec((L, LANES), tok),
                  pl.BlockSpec((1, SUBLANES, cd), lambda b, c: (b, 0, 0)),
                  pl.BlockSpec((1, DS, N), lambda b, c: (b, 0, 0)),
                  pl.BlockSpec((CW, cd), const2),
                  pl.BlockSpec((1, cd), const2),
                  pl.BlockSpec((1, LANES), const2),
                  pl.BlockSpec((1, LANES), const2),
                  pl.BlockSpec((1, DS), const2),
                  pl.BlockSpec((1, DS), const2),
                  pl.BlockSpec((LANES, DS), const2),
                  pl.BlockSpec((3 * LANES, LANES), const2)],
        out_specs=[pl.BlockSpec((L, DS), tok),
                   pl.BlockSpec((1, HF, L), lambda b, c: (b, 0, c)),
                   pl.BlockSpec((L, LANES), tok),
                   pl.BlockSpec((1, DS, N), lambda b, c: (b, 0, 0))],
        out_shape=[jax.ShapeDtypeStruct((B * T, DS), BF16),
                   jax.ShapeDtypeStruct((B, HF, T), F32),
                   jax.ShapeDtypeStruct((B * T, LANES), BF16),
                   jax.ShapeDtypeStruct((B, DS, N), F32)],
        scratch_shapes=[pltpu.VMEM((L + SUBLANES, cd), F32),
                        pltpu.VMEM((N, DS), F32),
                        pltpu.VMEM((1, LANES), F32)],
        compiler_params=_cparams(("arbitrary", "arbitrary")),
        name="ssd_prompt",
    )(xbc, z, fdt, hist8, init, conv_w, conv_b, bias128, alog128, dsk_e, g_ssm, e_mat, perm)


def _attn_kernel(qt_ref, k_ref, aug_ref, vt_ref, fg_ref, o_ref,
                 sa_ref, sb_ref, m_ref, l_ref, acc_ref, qa_ref, *, t, nq, dh, HF):
    hp = pl.program_id(1)
    rowp = lax.broadcasted_iota(jnp.int32, (2 * dh, 1), 0)
    first = rowp < dh
    for h in range(2):
        head = 2 * hp + h
        hit = jnp.logical_or(rowp == head, jnp.logical_or(rowp == head + HF, rowp == head + 2 * HF))
        sel = jnp.broadcast_to(jnp.where(hit, 1.0, 0.0), (2 * dh, t)).astype(BF16)
        for slot in range(2):
            qa_ref[slot, h, 2 * dh:4 * dh, :] = sel

    def load_q(qi, slot):
        qf = qt_ref[0, qi].astype(F32)
        qa_ref[slot, 0, 0:2 * dh, :] = jnp.where(first, qf, 0.0).astype(BF16)
        qa_ref[slot, 1, 0:2 * dh, :] = jnp.where(first, 0.0, qf).astype(BF16)

    def scores(slot, ki, s_ref):
        off = pl.multiple_of(ki * t, t)
        ka = jnp.concatenate([k_ref[pl.ds(off, t), :], aug_ref[pl.ds(off, t), :]], axis=1)
        for h in range(2):
            s_ref[h] = _dot(ka, qa_ref[slot, h])

    def absorb(ki, s_ref, mask):
        vt = vt_ref[0, ki]
        for h in range(2):
            s = s_ref[h]
            if mask is not None:
                s = jnp.where(mask, s, -jnp.inf)
            m = m_ref[h]
            m_new = jnp.maximum(m, jnp.max(s, axis=0, keepdims=True))
            alpha = jnp.exp2(m - m_new)
            p = jnp.exp2(s - m_new)
            l_ref[h] = alpha * l_ref[h] + jnp.sum(p, axis=0, keepdims=True)
            m_ref[h] = m_new
            acc_ref[h] = alpha * acc_ref[h] + _dot(vt, p.astype(BF16))

    r_i = lax.broadcasted_iota(jnp.int32, (t, t), 0)
    c_i = lax.broadcasted_iota(jnp.int32, (t, t), 1)
    diag = r_i <= c_i

    def q_block(qi, u):
        slot = u % 2
        x_ref, y_ref = (sa_ref, sb_ref) if u in (0, 3) else (sb_ref, sa_ref)
        m_ref[...] = jnp.full(m_ref.shape, -jnp.inf, F32)
        l_ref[...] = jnp.zeros(l_ref.shape, F32)
        acc_ref[...] = jnp.zeros(acc_ref.shape, F32)

        def pair(pp, carry):
            ki = 2 * pp
            scores(slot, ki + 1, y_ref)
            absorb(ki, x_ref, None)
            scores(slot, ki + 2, x_ref)
            absorb(ki + 1, y_ref, None)
            return carry

        lax.fori_loop(0, qi // 2, pair, 0)
        nxt = jnp.minimum(qi + 1, nq - 1)
        if u % 2 == 0:
            load_q(nxt, 1 - slot)
            scores(1 - slot, 0, y_ref)
            absorb(qi, x_ref, diag)
        else:
            scores(slot, qi, y_ref)
            absorb(qi - 1, x_ref, None)
            load_q(nxt, 1 - slot)
            scores(1 - slot, 0, x_ref)
            absorb(qi, y_ref, diag)
        ot = jnp.where(first, acc_ref[0] / l_ref[0], acc_ref[1] / l_ref[1])
        rows = pl.ds(pl.multiple_of(qi * t, t), t)
        o_ref[rows, :] = (ot.T * _silu(fg_ref[rows, :].astype(F32))).astype(BF16)

    load_q(0, 0)
    scores(0, 0, sa_ref)
    if nq % 4 == 0:
        def group(g, carry):
            for u in range(4):
                q_block(4 * g + u, u)
            return carry

        lax.fori_loop(0, nq // 4, group, 0)
    else:
        for qi in range(nq):
            q_block(jnp.int32(qi), qi % 4)


def _attn_prompt(qt4, kb, aug, vt4, fg, *, B, T, HF, dh):
    t = qt4.shape[3]
    nq = T // t
    assert T % t == 0 and 2 * dh == LANES and HF % 2 == 0 and qt4.shape == (B, nq, HF * dh, t)
    assert nq % 4 == 0 or nq <= 8
    npair = HF // 2
    return pl.pallas_call(
        functools.partial(_attn_kernel, t=t, nq=nq, dh=dh, HF=HF),
        grid=(B, npair),
        in_specs=[pl.BlockSpec((1, nq, LANES, t), lambda b, hp: (b, 0, hp, 0)),
                  pl.BlockSpec((T, LANES), lambda b, hp: (b, hp)),
                  pl.BlockSpec((T, LANES), lambda b, hp: (b, 0)),
                  pl.BlockSpec((1, nq, LANES, t), lambda b, hp: (b, 0, hp, 0)),
                  pl.BlockSpec((T, LANES), lambda b, hp: (b, hp))],
        out_specs=pl.BlockSpec((T, LANES), lambda b, hp: (b, hp)),
        out_shape=jax.ShapeDtypeStruct((B * T, HF * dh), BF16),
        scratch_shapes=[pltpu.VMEM((2, t, t), F32), pltpu.VMEM((2, t, t), F32),
                        pltpu.VMEM((2, 1, t), F32), pltpu.VMEM((2, 1, t), F32),
                        pltpu.VMEM((2, LANES, t), F32), pltpu.VMEM((2, 2, 2 * LANES, t), BF16)],
        compiler_params=_cparams(("parallel", "parallel")),
        name="fox_prompt",
    )(qt4, kb, aug, vt4, fg)


def _sample_pre_kernel(xbc_ref, sconv_ref, fdt_ref, cw_ref, cb_ref, bias_ref, alog_ref, e_ref,
                       logf_ref, xs_ref, xdt_t_ref, ed_t_ref, bs_ref, cs_ref, nconv_ref,
                       *, H, P, G, N, HF, CW):
    DS = H * P
    cd = xbc_ref.shape[1]
    x = xbc_ref[...]
    rows = [sconv_ref[:, kk * cd:(kk + 1) * cd] for kk in range(CW - 1)] + [x]
    cw = cw_ref[...]
    conv = cb_ref[...]
    for kk in range(CW):
        conv = conv + rows[kk] * cw[kk:kk + 1, :]
    nconv_ref[...] = jnp.concatenate(rows[1:], axis=1)
    xc = _silu(conv)
    xs = xc[:, :DS]
    xs_ref[...] = xs
    bs_ref[...] = xc[:, DS:DS + G * N]
    cs_ref[...] = xc[:, DS + G * N:]

    lane = lax.broadcasted_iota(jnp.int32, (1, LANES), 1)
    is_dt = jnp.logical_and(lane >= HF, lane < HF + H)
    sp, lsg = _softplus_terms(fdt_ref[...] + bias_ref[...])
    logf_ref[...] = lsg[:, :HF]
    dt = jnp.where(is_dt, sp, 0.0)
    ed = jnp.where(is_dt, jnp.exp(dt * (-jnp.exp(alog_ref[...]))), 0.0)
    s = dt.shape[0]
    hi, mid, lo = _split3(jnp.concatenate([dt, ed], axis=0))
    ex = _dot(jnp.concatenate([hi, mid, lo], axis=0), e_ref[...])
    ex = ex[:2 * s] + ex[2 * s:4 * s] + ex[4 * s:]
    xdt_t_ref[...] = (xs * ex[:s]).T
    ed_t_ref[...] = ex[s:].T


def _sample_pre(xbc, sconv, fdt, conv_w, conv_b, bias128, alog128, e_mat, *, H, P, G, N, HF):
    s, cd = xbc.shape
    DS = H * P
    CW = conv_w.shape[0]
    kern = functools.partial(_sample_pre_kernel, H=H, P=P, G=G, N=N, HF=HF, CW=CW)
    return pl.pallas_call(
        kern,
        out_shape=[jax.ShapeDtypeStruct((s, HF), F32),
                   jax.ShapeDtypeStruct((s, DS), F32),
                   jax.ShapeDtypeStruct((DS, s), F32),
                   jax.ShapeDtypeStruct((DS, s), F32),
                   jax.ShapeDtypeStruct((s, G * N), F32),
                   jax.ShapeDtypeStruct((s, G * N), F32),
                   jax.ShapeDtypeStruct((s, (CW - 1) * cd), F32)],
        compiler_params=pltpu.CompilerParams(vmem_limit_bytes=VMEM_LIMIT),
        name="sample_pre",
    )(xbc, sconv, fdt, conv_w, conv_b, bias128, alog128, e_mat)


def _sample_state_kernel(s0_ref, xdt_t_ref, ed_t_ref, bs_ref, cs_ref, xs_ref, z_ref, dsk_ref, gssm_ref,
                         new_ref, y_ref, *, H, P, G, N):
    si = pl.program_id(0)
    DS = H * P
    DSG = DS // G
    ns = xdt_t_ref.shape[1]
    lane = lax.broadcasted_iota(jnp.int32, (1, ns), 1)
    pick = lane == si
    xcol = jnp.sum(jnp.where(pick, xdt_t_ref[...], 0.0), axis=1, keepdims=True)
    ecol = jnp.sum(jnp.where(pick, ed_t_ref[...], 0.0), axis=1, keepdims=True)
    bs = bs_ref[0]
    cs = cs_ref[0]
    ys = []
    for g in range(G):
        gs = slice(g * DSG, (g + 1) * DSG)
        new_g = s0_ref[0, gs, :] * ecol[gs] + xcol[gs] * bs[:, g * N:(g + 1) * N]
        new_ref[0, gs, :] = new_g
        chi, clo = _split2(cs[:, g * N:(g + 1) * N])
        cst = jnp.concatenate([chi, clo, jnp.zeros((SUBLANES - 2, N), BF16)], axis=0)
        nhi, nlo = _split2(new_g)
        r = _dot_nt(cst, nhi) + _dot_nt(cst, nlo)
        ys.append(r[0:1] + r[1:2])
    y = jnp.concatenate(ys, axis=1) + dsk_ref[...] * xs_ref[0]
    yg = y * _silu(z_ref[0].astype(F32))
    inv = lax.rsqrt(jnp.mean(yg * yg, axis=-1, keepdims=True) + RMS_EPS)
    y_ref[0] = ((yg * inv) * gssm_ref[...]).astype(BF16)


def _sample_state(s0, xdt_t, ed_t, bs, cs, xs, z, dsk_e, g_ssm, *, H, P, G, N):
    ns, DS, _ = s0.shape
    row3 = lambda a: a.reshape(ns, 1, a.shape[-1])
    per = lambda w: pl.BlockSpec((1, 1, w), lambda s: (s, 0, 0))
    const2 = lambda s: (0, 0)
    new, y = pl.pallas_call(
        functools.partial(_sample_state_kernel, H=H, P=P, G=G, N=N),
        grid=(ns,),
        in_specs=[pl.BlockSpec((1, DS, N), lambda s: (s, 0, 0)),
                  pl.BlockSpec((DS, ns), const2),
                  pl.BlockSpec((DS, ns), const2),
                  per(G * N), per(G * N), per(DS), per(DS),
                  pl.BlockSpec((1, DS), const2),
                  pl.BlockSpec((1, DS), const2)],
        out_specs=[pl.BlockSpec((1, DS, N), lambda s: (s, 0, 0)), per(DS)],
        out_shape=[jax.ShapeDtypeStruct((ns, DS, N), F32),
                   jax.ShapeDtypeStruct((ns, 1, DS), BF16)],
        compiler_params=_cparams(("parallel",)),
        name="sample_state",
    )(s0, xdt_t, ed_t, row3(bs), row3(cs), row3(xs), row3(z), dsk_e, g_ssm)
    return new, y.reshape(ns, DS)


_PAGES_PER_STEP = 16


def _paged_kernel(pt_ref, q_ref, kn_ref, vn_ref, lfn_ref, fg_ref, ind_ref, *rest, PP, HF, DF):
    k_refs = rest[0:PP]
    v_refs = rest[PP:2 * PP]
    lf_refs = rest[2 * PP:3 * PP]
    o_ref = rest[3 * PP]
    m_ref, l_ref, r_ref, acc_ref, qrow_ref = rest[3 * PP + 1:]
    j = pl.program_id(1)
    nj = pl.num_programs(1)
    ps = k_refs[0].shape[2]
    ind = ind_ref[...]

    @pl.when(j == 0)
    def _():
        qrow = ind * q_ref[0].astype(F32)
        qrow_ref[...] = qrow.astype(BF16)
        m_ref[...] = jnp.sum(qrow * kn_ref[0].astype(F32), axis=1, keepdims=True)
        l_ref[...] = jnp.ones_like(l_ref)
        r_ref[...] = lfn_ref[0]
        acc_ref[...] = ind * vn_ref[0].astype(F32)

    kcat = jnp.concatenate([k_refs[r][0].astype(BF16) for r in range(PP)], axis=1)
    s = _dot(qrow_ref[...], kcat)

    lfs = [lf_refs[r][0] for r in range(PP)]
    r_i = lax.broadcasted_iota(jnp.int32, (ps, ps), 0)
    c_i = lax.broadcasted_iota(jnp.int32, (ps, ps), 1)
    later = jnp.where(r_i > c_i, 1.0, 0.0).astype(BF16)
    hi, mid, lo = _split3(jnp.concatenate(lfs, axis=0))
    n = PP * HF
    suf3 = _dot(jnp.concatenate([hi, mid, lo], axis=0), later)
    suf = suf3[:n] + suf3[n:2 * n] + suf3[2 * n:]
    rcur = r_ref[...]
    biases = []
    for r in range(PP):
        biases.append(suf[r * HF:(r + 1) * HF] + rcur)
        rcur = rcur + jnp.sum(lfs[r], axis=1, keepdims=True)
    r_ref[...] = rcur
    s = s + jnp.concatenate(biases, axis=1) * LOG2E

    m_old = m_ref[...]
    m_new = jnp.maximum(m_old, jnp.max(s, axis=1, keepdims=True))
    alpha = jnp.exp2(m_old - m_new)
    p = jnp.exp2(s - m_new)
    l_ref[...] = alpha * l_ref[...] + jnp.sum(p, axis=1, keepdims=True)
    m_ref[...] = m_new
    vcat = jnp.concatenate([v_refs[r][0].astype(BF16) for r in range(PP)], axis=1)
    acc_ref[...] = alpha * acc_ref[...] + _dot_nt(p.astype(BF16), vcat)

    @pl.when(j == nj - 1)
    def _():
        o = jnp.sum(acc_ref[...] * ind / l_ref[...], axis=0, keepdims=True)
        o_ref[0] = (o * _silu(fg_ref[0].astype(F32))).astype(BF16)


def _paged_attn(page_idx, q, kn, vn, lfn, fg, ind, ckt, cvt, clft, *, HF, DF):
    ns, npages = page_idx.shape
    ps = ckt.shape[2]
    PP = min(_PAGES_PER_STEP, npages)
    assert npages % PP == 0
    nj = npages // PP
    row3 = lambda a: a.reshape(ns, 1, a.shape[-1])
    per = lambda w: pl.BlockSpec((1, 1, w), lambda s, j, pt: (s, 0, 0))

    def page_spec(r, rows):
        return pl.BlockSpec((1, rows, ps), lambda s, j, pt: (pt[s, npages - 1 - (j * PP + r)], 0, 0))

    grid_spec = pltpu.PrefetchScalarGridSpec(
        num_scalar_prefetch=1,
        grid=(ns, nj),
        in_specs=[per(DF), per(DF), per(DF),
                  pl.BlockSpec((1, HF, 1), lambda s, j, pt: (s, 0, 0)),
                  per(DF),
                  pl.BlockSpec((HF, DF), lambda s, j, pt: (0, 0))]
                 + [page_spec(r, DF) for r in range(PP)]
                 + [page_spec(r, DF) for r in range(PP)]
                 + [page_spec(r, HF) for r in range(PP)],
        out_specs=per(DF),
        scratch_shapes=[pltpu.VMEM((HF, 1), F32), pltpu.VMEM((HF, 1), F32), pltpu.VMEM((HF, 1), F32),
                        pltpu.VMEM((HF, DF), F32), pltpu.VMEM((HF, DF), BF16)],
    )
    out = pl.pallas_call(
        functools.partial(_paged_kernel, PP=PP, HF=HF, DF=DF),
        grid_spec=grid_spec,
        out_shape=jax.ShapeDtypeStruct((ns, 1, DF), BF16),
        compiler_params=_cparams(("parallel", "arbitrary")),
        name="fox_paged",
    )(page_idx, row3(q), row3(kn), row3(vn), lfn.reshape(ns, HF, 1), row3(fg), ind,
      *([ckt] * PP), *([cvt] * PP), *([clft] * PP))
    return out.reshape(ns, DF)


def _final_kernel(ys_ref, yf_ref, ga_ref, gb_ref, x_ref, pe_ref,
                  wso_ref, wfo_ref, wo_ref, wpg_ref, wple_ref, gpost_ref, gple_ref, o_ref):
    a = _dot(ys_ref[...], wso_ref[...])
    b = _dot(yf_ref[...], wfo_ref[...])
    m = jax.nn.sigmoid(ga_ref[...].astype(F32)) * a + jax.nn.sigmoid(gb_ref[...].astype(F32)) * b
    o = _dot(m.astype(BF16), wo_ref[...])
    inv = lax.rsqrt(jnp.mean(o * o, axis=-1, keepdims=True) + RMS_EPS)
    x1 = x_ref[...] + (o * inv) * gpost_ref[...]
    inv1 = lax.rsqrt(jnp.mean(x1 * x1, axis=-1, keepdims=True) + RMS_EPS)
    hg = ((x1 * inv1) * gple_ref[...]).astype(BF16)
    gate = jax.nn.sigmoid(_dot(hg, wpg_ref[...]))
    pe = _dot(pe_ref[...].astype(BF16), wple_ref[...])
    o_ref[...] = x1 + gate * pe


def _final(ys, yf, ga, gb, x2, pemb, wso, wfo, wo, wpg, wple, g_post, g_ple):
    m, d = x2.shape
    tm = min(512, m)
    assert m % tm == 0
    tok = lambda w: pl.BlockSpec((tm, w), lambda i: (i, 0))
    full = lambda a: pl.BlockSpec(a.shape, lambda i: (0, 0))
    gp = g_post.reshape(1, d)
    gl = g_ple.reshape(1, d)
    return pl.pallas_call(
        _final_kernel,
        grid=(m // tm,),
        in_specs=[tok(ys.shape[1]), tok(yf.shape[1]), tok(d), tok(d), tok(d), tok(pemb.shape[1]),
                  full(wso), full(wfo), full(wo), full(wpg), full(wple), full(gp), full(gl)],
        out_specs=tok(d),
        out_shape=jax.ShapeDtypeStruct((m, d), F32),
        compiler_params=_cparams(("parallel",)),
        name="merge_out",
    )(ys, yf, ga, gb, x2, pemb, wso, wfo, wo, wpg, wple, gp, gl)


def kernel(x_prompt, x_sample, cache_k, cache_v, cache_logf, state_ssm, state_conv, page_table, p_prompt, p_sample, g_pre, w_in, conv_w, conv_b, dt_bias, a_log, d_skip, g_ssm, b_fgate, w_ssm_out, w_fox_out, w_out, g_post, w_ple, w_ple_gate, g_ple):
    B, T, D = x_prompt.shape
    S, TS, _ = x_sample.shape
    assert TS == 1
    depth = w_in.shape[0]
    _, n_phys, ps, HF, dh = cache_k.shape
    _, _, H, P, N = state_ssm.shape
    CW, cd = conv_w.shape[1], conv_w.shape[2]
    DS = H * P
    DF = HF * dh
    G = (cd - DS) // (2 * N)
    L = LANES
    q_scale = float(dh) ** -0.5 * LOG2E
    pad_l = LANES - HF - H

    ch_s = jnp.arange(DS) // P
    e_ssm = (jnp.arange(LANES)[:, None] == (HF + ch_s)[None, :]).astype(BF16)
    ind_fox = (jnp.arange(HF)[:, None] == (jnp.arange(DF) // dh)[None, :]).astype(F32)
    src = jnp.arange(3 * LANES)
    dst = (src // LANES) * HF + (src % LANES)
    perm = jnp.logical_and((src % LANES < HF)[:, None], dst[:, None] == jnp.arange(LANES)[None, :]).astype(BF16)

    ckt = jnp.transpose(cache_k, (0, 1, 3, 4, 2)).reshape(depth * n_phys, DF, ps)
    cvt = jnp.transpose(cache_v, (0, 1, 3, 4, 2)).reshape(depth * n_phys, DF, ps)
    clft = jnp.transpose(cache_logf, (0, 1, 3, 2)).reshape(depth * n_phys, HF, ps)

    xp = x_prompt.reshape(B * T, D)
    xs_tok = x_sample.reshape(S, D)
    outs_p = ([], [], [], [], [])
    outs_s = ([], [], [], [], [])
    for i in range(depth):
        z_w, xbc_w, dt_w, q_w, k_w, v_w, f_w, fg_w, ga_w, gb_w = jnp.split(
            w_in[i], [DS, DS + cd, DS + cd + H, DS + cd + H + DF, DS + cd + H + 2 * DF,
                      DS + cd + H + 3 * DF, DS + cd + H + 3 * DF + HF, DS + cd + H + 4 * DF + HF,
                      DS + cd + H + 4 * DF + HF + D], axis=1)
        w_tok = jnp.concatenate([z_w, fg_w, ga_w, gb_w, xbc_w], axis=1).astype(BF16)
        tok_widths = (DS, DF, D, D, cd)
        w_qkv = jnp.concatenate([q_w, k_w, v_w], axis=1).astype(BF16)
        w_small = jnp.concatenate([f_w, dt_w, jnp.zeros((D, pad_l), F32)], axis=1).astype(BF16)
        bias128 = jnp.concatenate([b_fgate[i], dt_bias[i], jnp.zeros((pad_l,), F32)]).reshape(1, LANES)
        alog128 = jnp.concatenate([jnp.zeros((HF,), F32), a_log[i], jnp.zeros((pad_l,), F32)]).reshape(1, LANES)
        dsk_e = jnp.repeat(d_skip[i], P).reshape(1, DS)
        gssm = g_ssm[i].reshape(1, DS)
        cw_i = conv_w[i]
        cb_i = conv_b[i].reshape(1, cd)
        wso = w_ssm_out[i].astype(BF16)
        wfo = w_fox_out[i].astype(BF16)
        wo = w_out[i].astype(BF16)
        wpg = w_ple_gate[i].astype(BF16)
        wple = w_ple[i].astype(BF16)
        proj_tok = functools.partial(_inproj_tok, g_pre=g_pre[i], w_tok=w_tok, w_small=w_small,
                                     widths=tok_widths, dtypes=(BF16, BF16, BF16, BF16, F32))
        proj_qkv = functools.partial(_inproj_qkv, g_pre=g_pre[i], w_qkv=w_qkv, d_fox=DF, q_scale=q_scale)
        fin = functools.partial(_final, wso=wso, wfo=wfo, wo=wo, wpg=wpg, wple=wple,
                                g_post=g_post[i], g_ple=g_ple[i])

        z, fg, ga, gb, xbc, fdt = proj_tok(xp)
        _, qt4, kb, kt, _, vt, vt4 = proj_qkv(xp, B=B, T=T)
        hist8 = jnp.zeros((B, SUBLANES, cd), F32)
        init = jnp.zeros((B, DS, N), F32)
        y_ssm, logft, aug, ssm_fin = _ssd_prompt(xbc, z, fdt, hist8, init, cw_i, cb_i, bias128, alog128,
                                                 dsk_e, gssm, e_ssm, perm, B=B, T=T, L=L, H=H, P=P, G=G, N=N, HF=HF)
        y_fox = _attn_prompt(qt4, kb, aug, vt4, fg, B=B, T=T, HF=HF, dh=dh)
        xp = fin(y_ssm, y_fox, ga, gb, xp, p_prompt[i].reshape(B * T, -1))
        xbc3 = xbc.reshape(B, T, cd)
        hist_p = jnp.zeros((B, CW - 1, cd), F32)
        conv_p = jnp.concatenate([hist_p, xbc3], axis=1)[:, T:] if T < CW - 1 else xbc3[:, T - (CW - 1):]
        k_out = jnp.transpose(kt.reshape(B, HF, dh, T), (0, 3, 1, 2))
        v_out = jnp.transpose(vt.reshape(B, HF, dh, T), (0, 3, 1, 2))
        for lst, arr in zip(outs_p, (k_out, v_out, jnp.transpose(logft, (0, 2, 1)),
                                     ssm_fin.reshape(B, H, P, N), conv_p)):
            lst.append(arr)

        z, fg, ga, gb, xbc, fdt = proj_tok(xs_tok)
        qb, _, kb, kt, vb, vt, _ = proj_qkv(xs_tok, B=1, T=S)
        logf_s, xs_c, xdt_t, ed_t, bs, cs, nconv = _sample_pre(
            xbc, state_conv[i].reshape(S, (CW - 1) * cd), fdt, cw_i, cb_i, bias128, alog128, e_ssm,
            H=H, P=P, G=G, N=N, HF=HF)
        ssm_new, y_ssm = _sample_state(state_ssm[i].reshape(S, DS, N), xdt_t, ed_t, bs, cs, xs_c, z,
                                       dsk_e, gssm, H=H, P=P, G=G, N=N)
        y_fox = _paged_attn(page_table + i * n_phys, qb, kb, vb, logf_s, fg, ind_fox, ckt, cvt, clft, HF=HF, DF=DF)
        xs_tok = fin(y_ssm, y_fox, ga, gb, xs_tok, p_sample[i].reshape(S, -1))
        k_out = jnp.transpose(kt.reshape(HF, dh, S), (2, 0, 1)).reshape(S, 1, HF, dh)
        v_out = jnp.transpose(vt.reshape(HF, dh, S), (2, 0, 1)).reshape(S, 1, HF, dh)
        for lst, arr in zip(outs_s, (k_out, v_out, logf_s.reshape(S, 1, HF),
                                     ssm_new.reshape(S, H, P, N), nconv.reshape(S, CW - 1, cd))):
            lst.append(arr)

    k_p, v_p, lf_p, ssm_p, conv_p = [jnp.stack(a) for a in outs_p]
    k_s, v_s, lf_s, ssm_s, conv_s = [jnp.stack(a) for a in outs_s]
    return (xp.reshape(B, T, D), xs_tok.reshape(S, 1, D), k_p, v_p, lf_p, ssm_p, conv_p,
            k_s, v_s, lf_s, ssm_s, conv_s)
```

```python
import functools
import math

import jax
import jax.numpy as jnp
from jax import lax
from jax.experimental import pallas as pl
from jax.experimental.pallas import tpu as pltpu

F32 = jnp.float32
BF16 = jnp.bfloat16
RMS_EPS = 1e-6
LOG2E = math.log2(math.e)
LANES = 128
SUBLANES = 8
VMEM_LIMIT = 56 * 1024 * 1024


def _cparams(sem):
    return pltpu.CompilerParams(dimension_semantics=sem, vmem_limit_bytes=VMEM_LIMIT)


def _silu(x):
    return x * jax.nn.sigmoid(x)


def _softplus_terms(x):
    t = jnp.log1p(jnp.exp(-jnp.abs(x)))
    return jnp.maximum(x, 0.0) + t, jnp.minimum(x, 0.0) - t


def _split2(x):
    hi = x.astype(BF16)
    lo = (x - hi.astype(F32)).astype(BF16)
    return hi, lo


def _split3(x):
    hi = x.astype(BF16)
    r = x - hi.astype(F32)
    mid = r.astype(BF16)
    lo = (r - mid.astype(F32)).astype(BF16)
    return hi, mid, lo


def _dot(a, b):
    return jnp.dot(a, b, preferred_element_type=F32)


def _dot_nt(a, b):
    return lax.dot_general(a, b, (((1,), (1,)), ((), ())), preferred_element_type=F32)


def _prenorm(x_ref, g_ref):
    x = x_ref[...]
    inv = lax.rsqrt(jnp.mean(x * x, axis=-1, keepdims=True) + RMS_EPS)
    return ((x * inv) * g_ref[...]).astype(BF16)


_IN_TN = 512
_IN_TM = 1024
_ATT_T = 512


def _inproj_tok_kernel(x_ref, g_ref, w_ref, ws_ref, z_ref, fg_ref, ga_ref, gb_ref, xbc_ref, fdt_ref,
                       h_ref, *, seg_starts):
    j = pl.program_id(1)

    @pl.when(j == 0)
    def _():
        hb = _prenorm(x_ref, g_ref)
        h_ref[...] = hb
        fdt_ref[...] = _dot(hb, ws_ref[...])

    outs = (z_ref, fg_ref, ga_ref, gb_ref, xbc_ref)
    for si, o_ref in enumerate(outs):
        @pl.when(jnp.logical_and(j >= seg_starts[si], j < seg_starts[si + 1]))
        def _(o_ref=o_ref):
            o_ref[...] = _dot(h_ref[...], w_ref[...]).astype(o_ref.dtype)


def _inproj_tok(x2, g_pre, w_tok, w_small, *, widths, dtypes):
    m, d = x2.shape
    tn = _IN_TN
    tm = min(_IN_TM, m)
    assert m % tm == 0 and all(w % tn == 0 for w in widths)
    starts = [0]
    for w in widths:
        starts.append(starts[-1] + w // tn)
    nj = starts[-1]
    assert w_tok.shape == (d, nj * tn)

    def seg_spec(si):
        lo, nb = starts[si], widths[si] // tn
        return pl.BlockSpec((tm, tn), lambda i, j: (i, jnp.clip(j - lo, 0, nb - 1)))

    return pl.pallas_call(
        functools.partial(_inproj_tok_kernel, seg_starts=tuple(starts)),
        grid=(m // tm, nj),
        in_specs=[pl.BlockSpec((tm, d), lambda i, j: (i, 0)),
                  pl.BlockSpec((1, d), lambda i, j: (0, 0)),
                  pl.BlockSpec((d, tn), lambda i, j: (0, j)),
                  pl.BlockSpec((d, LANES), lambda i, j: (0, 0))],
        out_specs=[seg_spec(si) for si in range(len(widths))]
                  + [pl.BlockSpec((tm, LANES), lambda i, j: (i, 0))],
        out_shape=[jax.ShapeDtypeStruct((m, w), dt) for w, dt in zip(widths, dtypes)]
                  + [jax.ShapeDtypeStruct((m, LANES), F32)],
        scratch_shapes=[pltpu.VMEM((tm, d), BF16)],
        compiler_params=_cparams(("parallel", "arbitrary")),
        name="inproj_tok",
    )(x2, g_pre.reshape(1, d), w_tok, w_small)


def _inproj_qkv_kernel(x_ref, g_ref, w_ref, qb_ref, qt_ref, kb_ref, kt_ref, vb_ref, vt_ref, vtb_ref,
                       h_ref, *, nb, nch, tch, q_scale):
    j = pl.program_id(2)

    @pl.when(j == 0)
    def _():
        h_ref[...] = _prenorm(x_ref, g_ref)

    @pl.when(j < nb)
    def _():
        r = _dot(h_ref[...], w_ref[...]) * q_scale
        qb_ref[...] = r.astype(BF16)
        rt = r.T.astype(BF16)
        for c in range(nch):
            qt_ref[0, c] = rt[:, c * tch:(c + 1) * tch]

    @pl.when(jnp.logical_and(j >= nb, j < 2 * nb))
    def _():
        r = _dot(h_ref[...], w_ref[...])
        kb_ref[...] = r.astype(BF16)
        kt_ref[0] = r.T

    @pl.when(j >= 2 * nb)
    def _():
        r = _dot(h_ref[...], w_ref[...])
        vb_ref[...] = r.astype(BF16)
        rt = r.T
        vt_ref[0] = rt
        rtb = rt.astype(BF16)
        for c in range(nch):
            vtb_ref[0, c] = rtb[:, c * tch:(c + 1) * tch]


def _inproj_qkv(x2, g_pre, w_qkv, *, B, T, d_fox, q_scale):
    m, d = x2.shape
    tn = _IN_TN
    tch = min(_ATT_T, T)
    tm = min(_IN_TM, T)
    assert T % tm == 0 and tm % tch == 0 and d_fox % tn == 0 and m == B * T
    nt = T // tm
    nch = tm // tch
    nb = d_fox // tn
    assert w_qkv.shape == (d, 3 * d_fox)

    def tok_spec(seg):
        return pl.BlockSpec((tm, tn), lambda b, t, j: (b * nt + t, jnp.clip(j - seg * nb, 0, nb - 1)))

    def tr_spec(seg):
        return pl.BlockSpec((1, tn, tm), lambda b, t, j: (b, jnp.clip(j - seg * nb, 0, nb - 1), t))

    def trc_spec(seg):
        return pl.BlockSpec((1, nch, tn, tch), lambda b, t, j: (b, t, jnp.clip(j - seg * nb, 0, nb - 1), 0))

    tok_shape = jax.ShapeDtypeStruct((m, d_fox), BF16)
    tr_shape = jax.ShapeDtypeStruct((B, d_fox, T), F32)
    trc_shape = jax.ShapeDtypeStruct((B, T // tch, d_fox, tch), BF16)
    return pl.pallas_call(
        functools.partial(_inproj_qkv_kernel, nb=nb, nch=nch, tch=tch, q_scale=q_scale),
        grid=(B, nt, 3 * nb),
        in_specs=[pl.BlockSpec((tm, d), lambda b, t, j: (b * nt + t, 0)),
                  pl.BlockSpec((1, d), lambda b, t, j: (0, 0)),
                  pl.BlockSpec((d, tn), lambda b, t, j: (0, j))],
        out_specs=[tok_spec(0), trc_spec(0), tok_spec(1), tr_spec(1), tok_spec(2), tr_spec(2), trc_spec(2)],
        out_shape=[tok_shape, trc_shape, tok_shape, tr_shape, tok_shape, tr_shape, trc_shape],
        scratch_shapes=[pltpu.VMEM((tm, d), BF16)],
        compiler_params=_cparams(("parallel", "parallel", "arbitrary")),
        name="inproj_qkv",
    )(x2, g_pre.reshape(1, d), w_qkv)


def _ssd_kernel(xbc_ref, z_ref, fdt_ref, hist_ref, init_ref, cw_ref, cb_ref, bias_ref, alog_ref,
                dsk_ref, gssm_ref, e_ref, perm_ref,
                y_ref, logft_ref, aug_ref, fin_ref,
                xext_ref, st_ref, carry_ref, *, L, H, P, G, N, HF, CW):
    c = pl.program_id(1)
    nc = pl.num_programs(1)
    DS = H * P
    DSG = DS // G
    HG = H // G

    @pl.when(c == 0)
    def _():
        xext_ref[0:SUBLANES, :] = hist_ref[0]
        st_ref[...] = init_ref[0].T
        carry_ref[...] = jnp.zeros_like(carry_ref)

    xext_ref[SUBLANES:SUBLANES + L, :] = xbc_ref[...]
    cw = cw_ref[...]
    conv = cb_ref[...]
    for kk in range(CW):
        off = SUBLANES - (CW - 1) + kk
        conv = conv + xext_ref[off:off + L, :] * cw[kk:kk + 1, :]
    xext_ref[0:SUBLANES, :] = xext_ref[L:L + SUBLANES, :]
    xc = _silu(conv)
    xs = xc[:, :DS]
    bm = xc[:, DS:DS + G * N]
    cm = xc[:, DS + G * N:]

    lane = lax.broadcasted_iota(jnp.int32, (1, LANES), 1)
    is_f = lane < HF
    is_dt = jnp.logical_and(lane >= HF, lane < HF + H)
    sp, lsg = _softplus_terms(fdt_ref[...] + bias_ref[...])
    dt = jnp.where(is_dt, sp, 0.0)
    da = dt * (-jnp.exp(alog_ref[...]))
    lf = jnp.where(is_f, lsg, 0.0)
    rowi = lax.broadcasted_iota(jnp.int32, (L, 1), 0)
    comb = da + lf + jnp.where(rowi == 0, carry_ref[...], 0.0)

    r_i = lax.broadcasted_iota(jnp.int32, (L, L), 0)
    c_i = lax.broadcasted_iota(jnp.int32, (L, L), 1)
    causal = r_i >= c_i
    tril = jnp.where(causal, 1.0, 0.0).astype(BF16)
    hi, mid, lo = _split3(comb)
    cum3 = _dot(tril, jnp.concatenate([hi, mid, lo], axis=1))
    cum = cum3[:, :LANES] + cum3[:, LANES:2 * LANES] + cum3[:, 2 * LANES:]
    carry_ref[...] = jnp.where(is_f, cum[L - 1:L, :], 0.0)
    cum_t = cum.T
    logft_ref[0] = lsg.T[0:HF, :]
    bh, bmid, bl = _split3(jnp.where(is_f, cum * (-LOG2E), 0.0))
    aug_ref[...] = _dot(jnp.concatenate([bh, bmid, bl], axis=1), perm_ref[...]).astype(BF16)

    a_last = cum[L - 1:L, :]
    qq = jnp.concatenate([dt, jnp.where(is_dt, jnp.exp(cum), 0.0),
                          jnp.where(is_dt, jnp.exp(a_last - cum), 0.0)], axis=0)
    qhi, qlo = _split2(qq)
    ex = _dot(jnp.concatenate([qhi, qlo], axis=0), e_ref[...])
    ex = ex[:3 * L] + ex[3 * L:]
    dt_e = ex[0:L]
    ea_e = ex[L:2 * L]
    dte_e = ex[2 * L:3 * L]

    xdt = xs * dt_e
    xdte_b = (xdt * dte_e).astype(BF16)
    lane_p = lax.broadcasted_iota(jnp.int32, (1, 2 * P), 1)
    first_head = lane_p < P

    y_groups = []
    for g in range(G):
        gs = slice(g * DSG, (g + 1) * DSG)
        bg_t = bm[:, g * N:(g + 1) * N].T.astype(BF16)
        cg = cm[:, g * N:(g + 1) * N].astype(BF16)
        cb = _dot(cg, bg_t)
        s_t = st_ref[:, gs]
        y_off = _dot(cg, s_t.astype(BF16)) * ea_e[:, gs]
        y_pairs = []
        for jp in range(HG // 2):
            e0 = g * HG + 2 * jp
            ms = []
            for e in (e0, e0 + 1):
                col = HF + e
                seg = cum[:, col:col + 1] - cum_t[col:col + 1, :]
                ms.append((cb * jnp.exp(jnp.where(causal, seg, -jnp.inf))).astype(BF16))
            lhs = jnp.concatenate(ms, axis=1)
            xp = xdt[:, e0 * P:(e0 + 2) * P]
            rhs = jnp.concatenate([jnp.where(first_head, xp, 0.0).astype(BF16),
                                   jnp.where(first_head, 0.0, xp).astype(BF16)], axis=0)
            y_pairs.append(_dot(lhs, rhs))
        y_groups.append(jnp.concatenate(y_pairs, axis=1) + y_off)
        st_ref[:, gs] = s_t * ea_e[L - 1:L, gs] + _dot(bg_t, xdte_b[:, gs])
    y = jnp.concatenate(y_groups, axis=1) + dsk_ref[...] * xs

    yg = y * _silu(z_ref[...].astype(F32))
    inv = lax.rsqrt(jnp.mean(yg * yg, axis=-1, keepdims=True) + RMS_EPS)
    y_ref[...] = ((yg * inv) * gssm_ref[...]).astype(BF16)

    @pl.when(c == nc - 1)
    def _():
        fin_ref[0] = st_ref[...].T


def _ssd_prompt(xbc, z, fdt, hist8, init, conv_w, conv_b, bias128, alog128, dsk_e, g_ssm, e_mat, perm,
                *, B, T, L, H, P, G, N, HF):
    DS = H * P
    cd = xbc.shape[1]
    CW = conv_w.shape[0]
    nc = T // L
    assert T % L == 0 and L == LANES and 2 * P == LANES and H % (2 * G) == 0
    assert HF + H <= LANES and 3 * HF <= LANES
    kern = functools.partial(_ssd_kernel, L=L, H=H, P=P, G=G, N=N, HF=HF, CW=CW)
    tok = lambda b, c: (b * nc + c, 0)
    const2 = lambda b, c: (0, 0)
    return pl.pallas_call(
        kern,
        grid=(B, nc),
        in_specs=[pl.BlockSpec((L, cd), tok),
                  pl.BlockSpec((L, DS), tok),
                  pl.BlockSpec((L, LANES), tok),
                  pl.BlockSpec((1, SUBLANES, cd), lambda b, c: (b, 0, 0)),
                  pl.BlockSpec((1, DS, N), lambda b, c: (b, 0, 0)),
                  pl.BlockSpec((CW, cd), const2),
                  pl.BlockSpec((1, cd), const2),
                  pl.BlockSpec((1, LANES), const2),
                  pl.BlockSpec((1, LANES), const2),
                  pl.BlockSpec((1, DS), const2),
                  pl.BlockSpec((1, DS), const2),
                  pl.BlockSpec((LANES, DS), const2),
                  pl.BlockSpec((3 * LANES, LANES), const2)],
        out_specs=[pl.BlockSpec((L, DS), tok),
                   pl.BlockSpec((1, HF, L), lambda b, c: (b, 0, c)),
                   pl.BlockSpec((L, LANES), tok),
                   pl.BlockSpec((1, DS, N), lambda b, c: (b, 0, 0))],
        out_shape=[jax.ShapeDtypeStruct((B * T, DS), BF16),
                   jax.ShapeDtypeStruct((B, HF, T), F32),
                   jax.ShapeDtypeStruct((B * T, LANES), BF16),
                   jax.ShapeDtypeStruct((B, DS, N), F32)],
        scratch_shapes=[pltpu.VMEM((L + SUBLANES, cd), F32),
                        pltpu.VMEM((N, DS), F32),
                        pltpu.VMEM((1, LANES), F32)],
        compiler_params=_cparams(("arbitrary", "arbitrary")),
        name="ssd_prompt",
    )(xbc, z, fdt, hist8, init, conv_w, conv_b, bias128, alog128, dsk_e, g_ssm, e_mat, perm)


def _attn_kernel(qt_ref, k_ref, aug_ref, vt_ref, fg_ref, o_ref,
                 sa_ref, sb_ref, m_ref, l_ref, acc_ref, qa_ref, *, t, nq, dh, HF):
    hp = pl.program_id(1)
    rowp = lax.broadcasted_iota(jnp.int32, (2 * dh, 1), 0)
    first = rowp < dh
    for h in range(2):
        head = 2 * hp + h
        hit = jnp.logical_or(rowp == head, jnp.logical_or(rowp == head + HF, rowp == head + 2 * HF))
        sel = jnp.broadcast_to(jnp.where(hit, 1.0, 0.0), (2 * dh, t)).astype(BF16)
        for slot in range(2):
            qa_ref[slot, h, 2 * dh:4 * dh, :] = sel

    def load_q(qi, slot):
        qf = qt_ref[0, qi].astype(F32)
        qa_ref[slot, 0, 0:2 * dh, :] = jnp.where(first, qf, 0.0).astype(BF16)
        qa_ref[slot, 1, 0:2 * dh, :] = jnp.where(first, 0.0, qf).astype(BF16)

    def scores(slot, ki, s_ref):
        off = pl.multiple_of(ki * t, t)
        ka = jnp.concatenate([k_ref[pl.ds(off, t), :], aug_ref[pl.ds(off, t), :]], axis=1)
        for h in range(2):
            s_ref[h] = _dot(ka, qa_ref[slot, h])

    def absorb(ki, s_ref, mask):
        vt = vt_ref[0, ki]
        for h in range(2):
            s = s_ref[h]
            if mask is not None:
                s = jnp.where(mask, s, -jnp.inf)
            m = m_ref[h]
            m_new = jnp.maximum(m, jnp.max(s, axis=0, keepdims=True))
            alpha = jnp.exp2(m - m_new)
            p = jnp.exp2(s - m_new)
            l_ref[h] = alpha * l_ref[h] + jnp.sum(p, axis=0, keepdims=True)
            m_ref[h] = m_new
            acc_ref[h] = alpha * acc_ref[h] + _dot(vt, p.astype(BF16))

    r_i = lax.broadcasted_iota(jnp.int32, (t, t), 0)
    c_i = lax.broadcasted_iota(jnp.int32, (t, t), 1)
    diag = r_i <= c_i

    def q_block(qi, u):
        slot = u % 2
        x_ref, y_ref = (sa_ref, sb_ref) if u in (0, 3) else (sb_ref, sa_ref)
        m_ref[...] = jnp.full(m_ref.shape, -jnp.inf, F32)
        l_ref[...] = jnp.zeros(l_ref.shape, F32)
        acc_ref[...] = jnp.zeros(acc_ref.shape, F32)

        def quad(qq, carry):
            ki = 4 * qq
            for r in range(4):
                cur, oth = (x_ref, y_ref) if r % 2 == 0 else (y_ref, x_ref)
                scores(slot, ki + r + 1, oth)
                absorb(ki + r, cur, None)
            return carry

        nquad = (qi - u) // 4
        lax.fori_loop(0, nquad, quad, 0)
        base = 4 * nquad
        cur, oth = x_ref, y_ref
        for r in range(u):
            scores(slot, base + r + 1, oth)
            absorb(base + r, cur, None)
            cur, oth = oth, cur
        nxt = jnp.minimum(qi + 1, nq - 1)
        load_q(nxt, 1 - slot)
        scores(1 - slot, 0, oth)
        absorb(qi, cur, diag)
        inv = [1.0 / l_ref[h] for h in range(2)]
        ot = jnp.where(first, acc_ref[0] * inv[0], acc_ref[1] * inv[1])
        rows = pl.ds(pl.multiple_of(qi * t, t), t)
        o_ref[rows, :] = (ot.T * _silu(fg_ref[rows, :].astype(F32))).astype(BF16)

    load_q(0, 0)
    scores(0, 0, sa_ref)
    if nq % 4 == 0:
        def group(g, carry):
            for u in range(4):
                q_block(4 * g + u, u)
            return carry

        lax.fori_loop(0, nq // 4, group, 0)
    else:
        for qi in range(nq):
            q_block(jnp.int32(qi), qi % 4)


def _attn_prompt(qt4, kb, aug, vt4, fg, *, B, T, HF, dh):
    t = qt4.shape[3]
    nq = T // t
    assert T % t == 0 and 2 * dh == LANES and HF % 2 == 0 and qt4.shape == (B, nq, HF * dh, t)
    assert nq % 4 == 0 or nq <= 8
    npair = HF // 2
    return pl.pallas_call(
        functools.partial(_attn_kernel, t=t, nq=nq, dh=dh, HF=HF),
        grid=(B, npair),
        in_specs=[pl.BlockSpec((1, nq, LANES, t), lambda b, hp: (b, 0, hp, 0)),
                  pl.BlockSpec((T, LANES), lambda b, hp: (b, hp)),
                  pl.BlockSpec((T, LANES), lambda b, hp: (b, 0)),
                  pl.BlockSpec((1, nq, LANES, t), lambda b, hp: (b, 0, hp, 0)),
                  pl.BlockSpec((T, LANES), lambda b, hp: (b, hp))],
        out_specs=pl.BlockSpec((T, LANES), lambda b, hp: (b, hp)),
        out_shape=jax.ShapeDtypeStruct((B * T, HF * dh), BF16),
        scratch_shapes=[pltpu.VMEM((2, t, t), F32), pltpu.VMEM((2, t, t), F32),
                        pltpu.VMEM((2, 1, t), F32), pltpu.VMEM((2, 1, t), F32),
                        pltpu.VMEM((2, LANES, t), F32), pltpu.VMEM((2, 2, 2 * LANES, t), BF16)],
        compiler_params=_cparams(("parallel", "parallel")),
        name="fox_prompt",
    )(qt4, kb, aug, vt4, fg)


def _sample_pre_kernel(xbc_ref, sconv_ref, fdt_ref, cw_ref, cb_ref, bias_ref, alog_ref, e_ref,
                       logf_ref, xs_ref, xdt_t_ref, ed_t_ref, bs_ref, cs_ref, nconv_ref,
                       *, H, P, G, N, HF, CW):
    DS = H * P
    cd = xbc_ref.shape[1]
    x = xbc_ref[...]
    rows = [sconv_ref[:, kk * cd:(kk + 1) * cd] for kk in range(CW - 1)] + [x]
    cw = cw_ref[...]
    conv = cb_ref[...]
    for kk in range(CW):
        conv = conv + rows[kk] * cw[kk:kk + 1, :]
    nconv_ref[...] = jnp.concatenate(rows[1:], axis=1)
    xc = _silu(conv)
    xs = xc[:, :DS]
    xs_ref[...] = xs
    bs_ref[...] = xc[:, DS:DS + G * N]
    cs_ref[...] = xc[:, DS + G * N:]

    lane = lax.broadcasted_iota(jnp.int32, (1, LANES), 1)
    is_dt = jnp.logical_and(lane >= HF, lane < HF + H)
    sp, lsg = _softplus_terms(fdt_ref[...] + bias_ref[...])
    logf_ref[...] = lsg[:, :HF]
    dt = jnp.where(is_dt, sp, 0.0)
    ed = jnp.where(is_dt, jnp.exp(dt * (-jnp.exp(alog_ref[...]))), 0.0)
    s = dt.shape[0]
    hi, mid, lo = _split3(jnp.concatenate([dt, ed], axis=0))
    ex = _dot(jnp.concatenate([hi, mid, lo], axis=0), e_ref[...])
    ex = ex[:2 * s] + ex[2 * s:4 * s] + ex[4 * s:]
    xdt_t_ref[...] = (xs * ex[:s]).T
    ed_t_ref[...] = ex[s:].T


def _sample_pre(xbc, sconv, fdt, conv_w, conv_b, bias128, alog128, e_mat, *, H, P, G, N, HF):
    s, cd = xbc.shape
    DS = H * P
    CW = conv_w.shape[0]
    kern = functools.partial(_sample_pre_kernel, H=H, P=P, G=G, N=N, HF=HF, CW=CW)
    return pl.pallas_call(
        kern,
        out_shape=[jax.ShapeDtypeStruct((s, HF), F32),
                   jax.ShapeDtypeStruct((s, DS), F32),
                   jax.ShapeDtypeStruct((DS, s), F32),
                   jax.ShapeDtypeStruct((DS, s), F32),
                   jax.ShapeDtypeStruct((s, G * N), F32),
                   jax.ShapeDtypeStruct((s, G * N), F32),
                   jax.ShapeDtypeStruct((s, (CW - 1) * cd), F32)],
        compiler_params=pltpu.CompilerParams(vmem_limit_bytes=VMEM_LIMIT),
        name="sample_pre",
    )(xbc, sconv, fdt, conv_w, conv_b, bias128, alog128, e_mat)


def _sample_state_kernel(s0_ref, xdt_t_ref, ed_t_ref, bs_ref, cs_ref, xs_ref, z_ref, dsk_ref, gssm_ref,
                         new_ref, y_ref, *, H, P, G, N):
    si = pl.program_id(0)
    DS = H * P
    DSG = DS // G
    ns = xdt_t_ref.shape[1]
    lane = lax.broadcasted_iota(jnp.int32, (1, ns), 1)
    pick = lane == si
    xcol = jnp.sum(jnp.where(pick, xdt_t_ref[...], 0.0), axis=1, keepdims=True)
    ecol = jnp.sum(jnp.where(pick, ed_t_ref[...], 0.0), axis=1, keepdims=True)
    bs = bs_ref[0]
    cs = cs_ref[0]
    ys = []
    for g in range(G):
        gs = slice(g * DSG, (g + 1) * DSG)
        new_g = s0_ref[0, gs, :] * ecol[gs] + xcol[gs] * bs[:, g * N:(g + 1) * N]
        new_ref[0, gs, :] = new_g
        chi, clo = _split2(cs[:, g * N:(g + 1) * N])
        cst = jnp.concatenate([chi, clo, jnp.zeros((SUBLANES - 2, N), BF16)], axis=0)
        nhi, nlo = _split2(new_g)
        r = _dot_nt(cst, nhi) + _dot_nt(cst, nlo)
        ys.append(r[0:1] + r[1:2])
    y = jnp.concatenate(ys, axis=1) + dsk_ref[...] * xs_ref[0]
    yg = y * _silu(z_ref[0].astype(F32))
    inv = lax.rsqrt(jnp.mean(yg * yg, axis=-1, keepdims=True) + RMS_EPS)
    y_ref[0] = ((yg * inv) * gssm_ref[...]).astype(BF16)


def _sample_state(s0, xdt_t, ed_t, bs, cs, xs, z, dsk_e, g_ssm, *, H, P, G, N):
    ns, DS, _ = s0.shape
    row3 = lambda a: a.reshape(ns, 1, a.shape[-1])
    per = lambda w: pl.BlockSpec((1, 1, w), lambda s: (s, 0, 0))
    const2 = lambda s: (0, 0)
    new, y = pl.pallas_call(
        functools.partial(_sample_state_kernel, H=H, P=P, G=G, N=N),
        grid=(ns,),
        in_specs=[pl.BlockSpec((1, DS, N), lambda s: (s, 0, 0)),
                  pl.BlockSpec((DS, ns), const2),
                  pl.BlockSpec((DS, ns), const2),
                  per(G * N), per(G * N), per(DS), per(DS),
                  pl.BlockSpec((1, DS), const2),
                  pl.BlockSpec((1, DS), const2)],
        out_specs=[pl.BlockSpec((1, DS, N), lambda s: (s, 0, 0)), per(DS)],
        out_shape=[jax.ShapeDtypeStruct((ns, DS, N), F32),
                   jax.ShapeDtypeStruct((ns, 1, DS), BF16)],
        compiler_params=_cparams(("parallel",)),
        name="sample_state",
    )(s0, xdt_t, ed_t, row3(bs), row3(cs), row3(xs), row3(z), dsk_e, g_ssm)
    return new, y.reshape(ns, DS)


_PAGES_PER_STEP = 16


def _paged_kernel(pt_ref, q_ref, kn_ref, vn_ref, lfn_ref, fg_ref, ind_ref, *rest, PP, HF, DF):
    k_refs = rest[0:PP]
    v_refs = rest[PP:2 * PP]
    lf_refs = rest[2 * PP:3 * PP]
    o_ref = rest[3 * PP]
    m_ref, l_ref, r_ref, acc_ref, qrow_ref = rest[3 * PP + 1:]
    j = pl.program_id(1)
    nj = pl.num_programs(1)
    ps = k_refs[0].shape[2]
    ind = ind_ref[...]

    @pl.when(j == 0)
    def _():
        qrow = ind * q_ref[0].astype(F32)
        qrow_ref[...] = qrow.astype(BF16)
        m_ref[...] = jnp.sum(qrow * kn_ref[0].astype(F32), axis=1, keepdims=True)
        l_ref[...] = jnp.ones_like(l_ref)
        r_ref[...] = lfn_ref[0]
        acc_ref[...] = ind * vn_ref[0].astype(F32)

    kcat = jnp.concatenate([k_refs[r][0].astype(BF16) for r in range(PP)], axis=1)
    s = _dot(qrow_ref[...], kcat)

    lfs = [lf_refs[r][0] for r in range(PP)]
    r_i = lax.broadcasted_iota(jnp.int32, (ps, ps), 0)
    c_i = lax.broadcasted_iota(jnp.int32, (ps, ps), 1)
    later = jnp.where(r_i > c_i, 1.0, 0.0).astype(BF16)
    hi, mid, lo = _split3(jnp.concatenate(lfs, axis=0))
    n = PP * HF
    suf3 = _dot(jnp.concatenate([hi, mid, lo], axis=0), later)
    suf = suf3[:n] + suf3[n:2 * n] + suf3[2 * n:]
    rcur = r_ref[...]
    biases = []
    for r in range(PP):
        biases.append(suf[r * HF:(r + 1) * HF] + rcur)
        rcur = rcur + jnp.sum(lfs[r], axis=1, keepdims=True)
    r_ref[...] = rcur
    s = s + jnp.concatenate(biases, axis=1) * LOG2E

    m_old = m_ref[...]
    m_new = jnp.maximum(m_old, jnp.max(s, axis=1, keepdims=True))
    alpha = jnp.exp2(m_old - m_new)
    p = jnp.exp2(s - m_new)
    l_ref[...] = alpha * l_ref[...] + jnp.sum(p, axis=1, keepdims=True)
    m_ref[...] = m_new
    vcat = jnp.concatenate([v_refs[r][0].astype(BF16) for r in range(PP)], axis=1)
    acc_ref[...] = alpha * acc_ref[...] + _dot_nt(p.astype(BF16), vcat)

    @pl.when(j == nj - 1)
    def _():
        o = jnp.sum(acc_ref[...] * ind / l_ref[...], axis=0, keepdims=True)
        o_ref[0] = (o * _silu(fg_ref[0].astype(F32))).astype(BF16)


def _paged_attn(page_idx, q, kn, vn, lfn, fg, ind, ckt, cvt, clft, *, HF, DF):
    ns, npages = page_idx.shape
    ps = ckt.shape[2]
    PP = min(_PAGES_PER_STEP, npages)
    assert npages % PP == 0
    nj = npages // PP
    row3 = lambda a: a.reshape(ns, 1, a.shape[-1])
    per = lambda w: pl.BlockSpec((1, 1, w), lambda s, j, pt: (s, 0, 0))

    def page_spec(r, rows):
        return pl.BlockSpec((1, rows, ps), lambda s, j, pt: (pt[s, npages - 1 - (j * PP + r)], 0, 0))

    grid_spec = pltpu.PrefetchScalarGridSpec(
        num_scalar_prefetch=1,
        grid=(ns, nj),
        in_specs=[per(DF), per(DF), per(DF),
                  pl.BlockSpec((1, HF, 1), lambda s, j, pt: (s, 0, 0)),
                  per(DF),
                  pl.BlockSpec((HF, DF), lambda s, j, pt: (0, 0))]
                 + [page_spec(r, DF) for r in range(PP)]
                 + [page_spec(r, DF) for r in range(PP)]
                 + [page_spec(r, HF) for r in range(PP)],
        out_specs=per(DF),
        scratch_shapes=[pltpu.VMEM((HF, 1), F32), pltpu.VMEM((HF, 1), F32), pltpu.VMEM((HF, 1), F32),
                        pltpu.VMEM((HF, DF), F32), pltpu.VMEM((HF, DF), BF16)],
    )
    out = pl.pallas_call(
        functools.partial(_paged_kernel, PP=PP, HF=HF, DF=DF),
        grid_spec=grid_spec,
        out_shape=jax.ShapeDtypeStruct((ns, 1, DF), BF16),
        compiler_params=_cparams(("parallel", "arbitrary")),
        name="fox_paged",
    )(page_idx, row3(q), row3(kn), row3(vn), lfn.reshape(ns, HF, 1), row3(fg), ind,
      *([ckt] * PP), *([cvt] * PP), *([clft] * PP))
    return out.reshape(ns, DF)


def _final_kernel(ys_ref, yf_ref, ga_ref, gb_ref, x_ref, pe_ref,
                  wso_ref, wfo_ref, wo_ref, wpg_ref, wple_ref, gpost_ref, gple_ref, o_ref):
    a = _dot(ys_ref[...], wso_ref[...])
    b = _dot(yf_ref[...], wfo_ref[...])
    m = jax.nn.sigmoid(ga_ref[...].astype(F32)) * a + jax.nn.sigmoid(gb_ref[...].astype(F32)) * b
    o = _dot(m.astype(BF16), wo_ref[...])
    inv = lax.rsqrt(jnp.mean(o * o, axis=-1, keepdims=True) + RMS_EPS)
    x1 = x_ref[...] + (o * inv) * gpost_ref[...]
    inv1 = lax.rsqrt(jnp.mean(x1 * x1, axis=-1, keepdims=True) + RMS_EPS)
    hg = ((x1 * inv1) * gple_ref[...]).astype(BF16)
    gate = jax.nn.sigmoid(_dot(hg, wpg_ref[...]))
    pe = _dot(pe_ref[...].astype(BF16), wple_ref[...])
    o_ref[...] = x1 + gate * pe


def _final(ys, yf, ga, gb, x2, pemb, wso, wfo, wo, wpg, wple, g_post, g_ple):
    m, d = x2.shape
    tm = min(512, m)
    assert m % tm == 0
    tok = lambda w: pl.BlockSpec((tm, w), lambda i: (i, 0))
    full = lambda a: pl.BlockSpec(a.shape, lambda i: (0, 0))
    gp = g_post.reshape(1, d)
    gl = g_ple.reshape(1, d)
    return pl.pallas_call(
        _final_kernel,
        grid=(m // tm,),
        in_specs=[tok(ys.shape[1]), tok(yf.shape[1]), tok(d), tok(d), tok(d), tok(pemb.shape[1]),
                  full(wso), full(wfo), full(wo), full(wpg), full(wple), full(gp), full(gl)],
        out_specs=tok(d),
        out_shape=jax.ShapeDtypeStruct((m, d), F32),
        compiler_params=_cparams(("parallel",)),
        name="merge_out",
    )(ys, yf, ga, gb, x2, pemb, wso, wfo, wo, wpg, wple, gp, gl)


def kernel(x_prompt, x_sample, cache_k, cache_v, cache_logf, state_ssm, state_conv, page_table, p_prompt, p_sample, g_pre, w_in, conv_w, conv_b, dt_bias, a_log, d_skip, g_ssm, b_fgate, w_ssm_out, w_fox_out, w_out, g_post, w_ple, w_ple_gate, g_ple):
    B, T, D = x_prompt.shape
    S, TS, _ = x_sample.shape
    assert TS == 1
    depth = w_in.shape[0]
    _, n_phys, ps, HF, dh = cache_k.shape
    _, _, H, P, N = state_ssm.shape
    CW, cd = conv_w.shape[1], conv_w.shape[2]
    DS = H * P
    DF = HF * dh
    G = (cd - DS) // (2 * N)
    L = LANES
    q_scale = float(dh) ** -0.5 * LOG2E
    pad_l = LANES - HF - H

    ch_s = jnp.arange(DS) // P
    e_ssm = (jnp.arange(LANES)[:, None] == (HF + ch_s)[None, :]).astype(BF16)
    ind_fox = (jnp.arange(HF)[:, None] == (jnp.arange(DF) // dh)[None, :]).astype(F32)
    src = jnp.arange(3 * LANES)
    dst = (src // LANES) * HF + (src % LANES)
    perm = jnp.logical_and((src % LANES < HF)[:, None], dst[:, None] == jnp.arange(LANES)[None, :]).astype(BF16)

    ckt = jnp.transpose(cache_k, (0, 1, 3, 4, 2)).reshape(depth * n_phys, DF, ps)
    cvt = jnp.transpose(cache_v, (0, 1, 3, 4, 2)).reshape(depth * n_phys, DF, ps)
    clft = jnp.transpose(cache_logf, (0, 1, 3, 2)).reshape(depth * n_phys, HF, ps)

    xp = x_prompt.reshape(B * T, D)
    xs_tok = x_sample.reshape(S, D)
    outs_p = ([], [], [], [], [])
    outs_s = ([], [], [], [], [])
    for i in range(depth):
        z_w, xbc_w, dt_w, q_w, k_w, v_w, f_w, fg_w, ga_w, gb_w = jnp.split(
            w_in[i], [DS, DS + cd, DS + cd + H, DS + cd + H + DF, DS + cd + H + 2 * DF,
                      DS + cd + H + 3 * DF, DS + cd + H + 3 * DF + HF, DS + cd + H + 4 * DF + HF,
                      DS + cd + H + 4 * DF + HF + D], axis=1)
        w_tok = jnp.concatenate([z_w, fg_w, ga_w, gb_w, xbc_w], axis=1).astype(BF16)
        tok_widths = (DS, DF, D, D, cd)
        w_qkv = jnp.concatenate([q_w, k_w, v_w], axis=1).astype(BF16)
        w_small = jnp.concatenate([f_w, dt_w, jnp.zeros((D, pad_l), F32)], axis=1).astype(BF16)
        bias128 = jnp.concatenate([b_fgate[i], dt_bias[i], jnp.zeros((pad_l,), F32)]).reshape(1, LANES)
        alog128 = jnp.concatenate([jnp.zeros((HF,), F32), a_log[i], jnp.zeros((pad_l,), F32)]).reshape(1, LANES)
        dsk_e = jnp.repeat(d_skip[i], P).reshape(1, DS)
        gssm = g_ssm[i].reshape(1, DS)
        cw_i = conv_w[i]
        cb_i = conv_b[i].reshape(1, cd)
        wso = w_ssm_out[i].astype(BF16)
        wfo = w_fox_out[i].astype(BF16)
        wo = w_out[i].astype(BF16)
        wpg = w_ple_gate[i].astype(BF16)
        wple = w_ple[i].astype(BF16)
        proj_tok = functools.partial(_inproj_tok, g_pre=g_pre[i], w_tok=w_tok, w_small=w_small,
                                     widths=tok_widths, dtypes=(BF16, BF16, BF16, BF16, F32))
        proj_qkv = functools.partial(_inproj_qkv, g_pre=g_pre[i], w_qkv=w_qkv, d_fox=DF, q_scale=q_scale)
        fin = functools.partial(_final, wso=wso, wfo=wfo, wo=wo, wpg=wpg, wple=wple,
                                g_post=g_post[i], g_ple=g_ple[i])

        z, fg, ga, gb, xbc, fdt = proj_tok(xp)
        _, qt4, kb, kt, _, vt, vt4 = proj_qkv(xp, B=B, T=T)
        hist8 = jnp.zeros((B, SUBLANES, cd), F32)
        init = jnp.zeros((B, DS, N), F32)
        y_ssm, logft, aug, ssm_fin = _ssd_prompt(xbc, z, fdt, hist8, init, cw_i, cb_i, bias128, alog128,
                                                 dsk_e, gssm, e_ssm, perm, B=B, T=T, L=L, H=H, P=P, G=G, N=N, HF=HF)
        y_fox = _attn_prompt(qt4, kb, aug, vt4, fg, B=B, T=T, HF=HF, dh=dh)
        xp = fin(y_ssm, y_fox, ga, gb, xp, p_prompt[i].reshape(B * T, -1))
        xbc3 = xbc.reshape(B, T, cd)
        hist_p = jnp.zeros((B, CW - 1, cd), F32)
        conv_p = jnp.concatenate([hist_p, xbc3], axis=1)[:, T:] if T < CW - 1 else xbc3[:, T - (CW - 1):]
        k_out = jnp.transpose(kt.reshape(B, HF, dh, T), (0, 3, 1, 2))
        v_out = jnp.transpose(vt.reshape(B, HF, dh, T), (0, 3, 1, 2))
        for lst, arr in zip(outs_p, (k_out, v_out, jnp.transpose(logft, (0, 2, 1)),
                                     ssm_fin.reshape(B, H, P, N), conv_p)):
            lst.append(arr)

        z, fg, ga, gb, xbc, fdt = proj_tok(xs_tok)
        qb, _, kb, kt, vb, vt, _ = proj_qkv(xs_tok, B=1, T=S)
        logf_s, xs_c, xdt_t, ed_t, bs, cs, nconv = _sample_pre(
            xbc, state_conv[i].reshape(S, (CW - 1) * cd), fdt, cw_i, cb_i, bias128, alog128, e_ssm,
            H=H, P=P, G=G, N=N, HF=HF)
        ssm_new, y_ssm = _sample_state(state_ssm[i].reshape(S, DS, N), xdt_t, ed_t, bs, cs, xs_c, z,
                                       dsk_e, gssm, H=H, P=P, G=G, N=N)
        y_fox = _paged_attn(page_table + i * n_phys, qb, kb, vb, logf_s, fg, ind_fox, ckt, cvt, clft, HF=HF, DF=DF)
        xs_tok = fin(y_ssm, y_fox, ga, gb, xs_tok, p_sample[i].reshape(S, -1))
        k_out = jnp.transpose(kt.reshape(HF, dh, S), (2, 0, 1)).reshape(S, 1, HF, dh)
        v_out = jnp.transpose(vt.reshape(HF, dh, S), (2, 0, 1)).reshape(S, 1, HF, dh)
        for lst, arr in zip(outs_s, (k_out, v_out, logf_s.reshape(S, 1, HF),
                                     ssm_new.reshape(S, H, P, N), nconv.reshape(S, CW - 1, cd))):
            lst.append(arr)

    k_p, v_p, lf_p, ssm_p, conv_p = [jnp.stack(a) for a in outs_p]
    k_s, v_s, lf_s, ssm_s, conv_s = [jnp.stack(a) for a in outs_s]
    return (xp.reshape(B, T, D), xs_tok.reshape(S, 1, D), k_p, v_p, lf_p, ssm_p, conv_p,
            k_s, v_s, lf_s, ssm_s, conv_s)
```

```python
import functools
import math

import jax
import jax.numpy as jnp
from jax import lax
from jax.experimental import pallas as pl
from jax.experimental.pallas import tpu as pltpu

F32 = jnp.float32
BF16 = jnp.bfloat16
RMS_EPS = 1e-6
LOG2E = math.log2(math.e)
LANES = 128
SUBLANES = 8
VMEM_LIMIT = 56 * 1024 * 1024


def _cparams(sem):
    return pltpu.CompilerParams(dimension_semantics=sem, vmem_limit_bytes=VMEM_LIMIT)


def _silu(x):
    return x * jax.nn.sigmoid(x)


def _softplus_terms(x):
    t = jnp.log1p(jnp.exp(-jnp.abs(x)))
    return jnp.maximum(x, 0.0) + t, jnp.minimum(x, 0.0) - t


def _split2(x):
    hi = x.astype(BF16)
    lo = (x - hi.astype(F32)).astype(BF16)
    return hi, lo


def _split3(x):
    hi = x.astype(BF16)
    r = x - hi.astype(F32)
    mid = r.astype(BF16)
    lo = (r - mid.astype(F32)).astype(BF16)
    return hi, mid, lo


def _dot(a, b):
    return jnp.dot(a, b, preferred_element_type=F32)


def _dot_nt(a, b):
    return lax.dot_general(a, b, (((1,), (1,)), ((), ())), preferred_element_type=F32)


def _prenorm(x_ref, g_ref):
    x = x_ref[...]
    inv = lax.rsqrt(jnp.mean(x * x, axis=-1, keepdims=True) + RMS_EPS)
    return ((x * inv) * g_ref[...]).astype(BF16)


_IN_TN = 512
_IN_TM = 1024
_ATT_T = 512


def _inproj_tok_kernel(x_ref, g_ref, w_ref, ws_ref, z_ref, fg_ref, ga_ref, gb_ref, xbc_ref, fdt_ref,
                       h_ref, *, seg_starts):
    j = pl.program_id(1)

    @pl.when(j == 0)
    def _():
        hb = _prenorm(x_ref, g_ref)
        h_ref[...] = hb
        fdt_ref[...] = _dot(hb, ws_ref[...])

    outs = (z_ref, fg_ref, ga_ref, gb_ref, xbc_ref)
    for si, o_ref in enumerate(outs):
        @pl.when(jnp.logical_and(j >= seg_starts[si], j < seg_starts[si + 1]))
        def _(o_ref=o_ref):
            o_ref[...] = _dot(h_ref[...], w_ref[...]).astype(o_ref.dtype)


def _inproj_tok(x2, g_pre, w_tok, w_small, *, widths, dtypes):
    m, d = x2.shape
    tn = _IN_TN
    tm = min(2 * _IN_TM, m)
    assert m % tm == 0 and all(w % tn == 0 for w in widths)
    starts = [0]
    for w in widths:
        starts.append(starts[-1] + w // tn)
    nj = starts[-1]
    assert w_tok.shape == (d, nj * tn)

    def seg_spec(si):
        lo, nb = starts[si], widths[si] // tn
        return pl.BlockSpec((tm, tn), lambda i, j: (i, jnp.clip(j - lo, 0, nb - 1)))

    return pl.pallas_call(
        functools.partial(_inproj_tok_kernel, seg_starts=tuple(starts)),
        grid=(m // tm, nj),
        in_specs=[pl.BlockSpec((tm, d), lambda i, j: (i, 0)),
                  pl.BlockSpec((1, d), lambda i, j: (0, 0)),
                  pl.BlockSpec((d, tn), lambda i, j: (0, j)),
                  pl.BlockSpec((d, LANES), lambda i, j: (0, 0))],
        out_specs=[seg_spec(si) for si in range(len(widths))]
                  + [pl.BlockSpec((tm, LANES), lambda i, j: (i, 0))],
        out_shape=[jax.ShapeDtypeStruct((m, w), dt) for w, dt in zip(widths, dtypes)]
                  + [jax.ShapeDtypeStruct((m, LANES), F32)],
        scratch_shapes=[pltpu.VMEM((tm, d), BF16)],
        compiler_params=_cparams(("parallel", "arbitrary")),
        name="inproj_tok",
    )(x2, g_pre.reshape(1, d), w_tok, w_small)


def _inproj_qkv_kernel(x_ref, g_ref, w_ref, qb_ref, qt_ref, kb_ref, kt_ref, vb_ref, vt_ref, vtb_ref,
                       h_ref, *, nb, nch, tch, q_scale):
    j = pl.program_id(2)

    @pl.when(j == 0)
    def _():
        h_ref[...] = _prenorm(x_ref, g_ref)

    @pl.when(j < nb)
    def _():
        r = _dot(h_ref[...], w_ref[...]) * q_scale
        qb_ref[...] = r.astype(BF16)
        rt = r.T.astype(BF16)
        for c in range(nch):
            qt_ref[0, c] = rt[:, c * tch:(c + 1) * tch]

    @pl.when(jnp.logical_and(j >= nb, j < 2 * nb))
    def _():
        r = _dot(h_ref[...], w_ref[...])
        kb_ref[...] = r.astype(BF16)
        kt_ref[0] = r.T

    @pl.when(j >= 2 * nb)
    def _():
        r = _dot(h_ref[...], w_ref[...])
        vb_ref[...] = r.astype(BF16)
        rt = r.T
        vt_ref[0] = rt
        rtb = rt.astype(BF16)
        for c in range(nch):
            vtb_ref[0, c] = rtb[:, c * tch:(c + 1) * tch]


def _inproj_qkv(x2, g_pre, w_qkv, *, B, T, d_fox, q_scale):
    m, d = x2.shape
    tn = _IN_TN
    tch = min(_ATT_T, T)
    tm = min(_IN_TM, T)
    assert T % tm == 0 and tm % tch == 0 and d_fox % tn == 0 and m == B * T
    nt = T // tm
    nch = tm // tch
    nb = d_fox // tn
    assert w_qkv.shape == (d, 3 * d_fox)

    def tok_spec(seg):
        return pl.BlockSpec((tm, tn), lambda b, t, j: (b * nt + t, jnp.clip(j - seg * nb, 0, nb - 1)))

    def tr_spec(seg):
        return pl.BlockSpec((1, tn, tm), lambda b, t, j: (b, jnp.clip(j - seg * nb, 0, nb - 1), t))

    def trc_spec(seg):
        return pl.BlockSpec((1, nch, tn, tch), lambda b, t, j: (b, t, jnp.clip(j - seg * nb, 0, nb - 1), 0))

    tok_shape = jax.ShapeDtypeStruct((m, d_fox), BF16)
    tr_shape = jax.ShapeDtypeStruct((B, d_fox, T), F32)
    trc_shape = jax.ShapeDtypeStruct((B, T // tch, d_fox, tch), BF16)
    return pl.pallas_call(
        functools.partial(_inproj_qkv_kernel, nb=nb, nch=nch, tch=tch, q_scale=q_scale),
        grid=(B, nt, 3 * nb),
        in_specs=[pl.BlockSpec((tm, d), lambda b, t, j: (b * nt + t, 0)),
                  pl.BlockSpec((1, d), lambda b, t, j: (0, 0)),
                  pl.BlockSpec((d, tn), lambda b, t, j: (0, j))],
        out_specs=[tok_spec(0), trc_spec(0), tok_spec(1), tr_spec(1), tok_spec(2), tr_spec(2), trc_spec(2)],
        out_shape=[tok_shape, trc_shape, tok_shape, tr_shape, tok_shape, tr_shape, trc_shape],
        scratch_shapes=[pltpu.VMEM((tm, d), BF16)],
        compiler_params=_cparams(("parallel", "parallel", "arbitrary")),
        name="inproj_qkv",
    )(x2, g_pre.reshape(1, d), w_qkv)


def _ssd_kernel(xbc_ref, z_ref, fdt_ref, hist_ref, init_ref, cw_ref, cb_ref, bias_ref, alog_ref,
                dsk_ref, gssm_ref, e_ref, perm_ref,
                y_ref, logft_ref, aug_ref, fin_ref,
                xext_ref, st_ref, carry_ref, *, L, H, P, G, N, HF, CW):
    c = pl.program_id(1)
    nc = pl.num_programs(1)
    DS = H * P
    DSG = DS // G
    HG = H // G

    @pl.when(c == 0)
    def _():
        xext_ref[0:SUBLANES, :] = hist_ref[0]
        st_ref[...] = init_ref[0].T
        carry_ref[...] = jnp.zeros_like(carry_ref)

    xext_ref[SUBLANES:SUBLANES + L, :] = xbc_ref[...]
    cw = cw_ref[...]
    conv = cb_ref[...]
    for kk in range(CW):
        off = SUBLANES - (CW - 1) + kk
        conv = conv + xext_ref[off:off + L, :] * cw[kk:kk + 1, :]
    xext_ref[0:SUBLANES, :] = xext_ref[L:L + SUBLANES, :]
    xc = _silu(conv)
    xs = xc[:, :DS]
    bm = xc[:, DS:DS + G * N]
    cm = xc[:, DS + G * N:]

    lane = lax.broadcasted_iota(jnp.int32, (1, LANES), 1)
    is_f = lane < HF
    is_dt = jnp.logical_and(lane >= HF, lane < HF + H)
    sp, lsg = _softplus_terms(fdt_ref[...] + bias_ref[...])
    dt = jnp.where(is_dt, sp, 0.0)
    da = dt * (-jnp.exp(alog_ref[...]))
    lf = jnp.where(is_f, lsg, 0.0)
    rowi = lax.broadcasted_iota(jnp.int32, (L, 1), 0)
    comb = da + lf + jnp.where(rowi == 0, carry_ref[...], 0.0)

    r_i = lax.broadcasted_iota(jnp.int32, (L, L), 0)
    c_i = lax.broadcasted_iota(jnp.int32, (L, L), 1)
    causal = r_i >= c_i
    tril = jnp.where(causal, 1.0, 0.0).astype(BF16)
    hi, mid, lo = _split3(comb)
    cum3 = _dot(tril, jnp.concatenate([hi, mid, lo], axis=1))
    cum = cum3[:, :LANES] + cum3[:, LANES:2 * LANES] + cum3[:, 2 * LANES:]
    carry_ref[...] = jnp.where(is_f, cum[L - 1:L, :], 0.0)
    cum_t = cum.T
    logft_ref[0] = lsg.T[0:HF, :]
    bh, bmid, bl = _split3(jnp.where(is_f, cum * (-LOG2E), 0.0))
    aug_ref[...] = _dot(jnp.concatenate([bh, bmid, bl], axis=1), perm_ref[...]).astype(BF16)

    a_last = cum[L - 1:L, :]
    qq = jnp.concatenate([dt, jnp.where(is_dt, jnp.exp(cum), 0.0),
                          jnp.where(is_dt, jnp.exp(a_last - cum), 0.0)], axis=0)
    qhi, qlo = _split2(qq)
    ex = _dot(jnp.concatenate([qhi, qlo], axis=0), e_ref[...])
    ex = ex[:3 * L] + ex[3 * L:]
    dt_e = ex[0:L]
    ea_e = ex[L:2 * L]
    dte_e = ex[2 * L:3 * L]

    xdt = xs * dt_e
    xdte_b = (xdt * dte_e).astype(BF16)
    lane_p = lax.broadcasted_iota(jnp.int32, (1, 2 * P), 1)
    first_head = lane_p < P

    y_groups = []
    for g in range(G):
        gs = slice(g * DSG, (g + 1) * DSG)
        bg_t = bm[:, g * N:(g + 1) * N].T.astype(BF16)
        cg = cm[:, g * N:(g + 1) * N].astype(BF16)
        cb = _dot(cg, bg_t)
        s_t = st_ref[:, gs]
        y_off = _dot(cg, s_t.astype(BF16)) * ea_e[:, gs]
        y_pairs = []
        for jp in range(HG // 2):
            e0 = g * HG + 2 * jp
            ms = []
            for e in (e0, e0 + 1):
                col = HF + e
                seg = cum[:, col:col + 1] - cum_t[col:col + 1, :]
                ms.append((cb * jnp.exp(jnp.where(causal, seg, -jnp.inf))).astype(BF16))
            lhs = jnp.concatenate(ms, axis=1)
            xp = xdt[:, e0 * P:(e0 + 2) * P]
            rhs = jnp.concatenate([jnp.where(first_head, xp, 0.0).astype(BF16),
                                   jnp.where(first_head, 0.0, xp).astype(BF16)], axis=0)
            y_pairs.append(_dot(lhs, rhs))
        y_groups.append(jnp.concatenate(y_pairs, axis=1) + y_off)
        st_ref[:, gs] = s_t * ea_e[L - 1:L, gs] + _dot(bg_t, xdte_b[:, gs])
    y = jnp.concatenate(y_groups, axis=1) + dsk_ref[...] * xs

    yg = y * _silu(z_ref[...].astype(F32))
    inv = lax.rsqrt(jnp.mean(yg * yg, axis=-1, keepdims=True) + RMS_EPS)
    y_ref[...] = ((yg * inv) * gssm_ref[...]).astype(BF16)

    @pl.when(c == nc - 1)
    def _():
        fin_ref[0] = st_ref[...].T


def _ssd_prompt(xbc, z, fdt, hist8, init, conv_w, conv_b, bias128, alog128, dsk_e, g_ssm, e_mat, perm,
                *, B, T, L, H, P, G, N, HF):
    DS = H * P
    cd = xbc.shape[1]
    CW = conv_w.shape[0]
    nc = T // L
    assert T % L == 0 and L == LANES and 2 * P == LANES and H % (2 * G) == 0
    assert HF + H <= LANES and 3 * HF <= LANES
    kern = functools.partial(_ssd_kernel, L=L, H=H, P=P, G=G, N=N, HF=HF, CW=CW)
    tok = lambda b, c: (b * nc + c, 0)
    const2 = lambda b, c: (0, 0)
    return pl.pallas_call(
        kern,
        grid=(B, nc),
        in_specs=[pl.BlockSpec((L, cd), tok),
                  pl.BlockSpec((L, DS), tok),
                  pl.BlockSpec((L, LANES), tok),
                  pl.BlockSpec((1, SUBLANES, cd), lambda b, c: (b, 0, 0)),
                  pl.BlockSpec((1, DS, N), lambda b, c: (b, 0, 0)),
                  pl.BlockSpec((CW, cd), const2),
                  pl.BlockSpec((1, cd), const2),
                  pl.BlockSpec((1, LANES), const2),
                  pl.BlockSpec((1, LANES), const2),
                  pl.BlockSpec((1, DS), const2),
                  pl.BlockSpec((1, DS), const2),
                  pl.BlockSpec((LANES, DS), const2),
                  pl.BlockSpec((3 * LANES, LANES), const2)],
        out_specs=[pl.BlockSpec((L, DS), tok),
                   pl.BlockSpec((1, HF, L), lambda b, c: (b, 0, c)),
                   pl.BlockSpec((L, LANES), tok),
                   pl.BlockSpec((1, DS, N), lambda b, c: (b, 0, 0))],
        out_shape=[jax.ShapeDtypeStruct((B * T, DS), BF16),
                   jax.ShapeDtypeStruct((B, HF, T), F32),
                   jax.ShapeDtypeStruct((B * T, LANES), BF16),
                   jax.ShapeDtypeStruct((B, DS, N), F32)],
        scratch_shapes=[pltpu.VMEM((L + SUBLANES, cd), F32),
                        pltpu.VMEM((N, DS), F32),
                        pltpu.VMEM((1, LANES), F32)],
        compiler_params=_cparams(("arbitrary", "arbitrary")),
        name="ssd_prompt",
    )(xbc, z, fdt, hist8, init, conv_w, conv_b, bias128, alog128, dsk_e, g_ssm, e_mat, perm)


def _attn_kernel(qt_ref, k_ref, aug_ref, vt_ref, fg_ref, o_ref,
                 sa_ref, sb_ref, m_ref, l_ref, acc_ref, qa_ref, *, t, nq, dh, HF):
    hp = pl.program_id(1)
    rowp = lax.broadcasted_iota(jnp.int32, (2 * dh, 1), 0)
    first = rowp < dh
    for h in range(2):
        head = 2 * hp + h
        hit = jnp.logical_or(rowp == head, jnp.logical_or(rowp == head + HF, rowp == head + 2 * HF))
        sel = jnp.broadcast_to(jnp.where(hit, 1.0, 0.0), (2 * dh, t)).astype(BF16)
        for slot in range(2):
            qa_ref[slot, h, 2 * dh:4 * dh, :] = sel

    def load_q(qi, slot):
        qf = qt_ref[0, qi].astype(F32)
        qa_ref[slot, 0, 0:2 * dh, :] = jnp.where(first, qf, 0.0).astype(BF16)
        qa_ref[slot, 1, 0:2 * dh, :] = jnp.where(first, 0.0, qf).astype(BF16)

    def scores(slot, ki, s_ref):
        off = pl.multiple_of(ki * t, t)
        ka = jnp.concatenate([k_ref[pl.ds(off, t), :], aug_ref[pl.ds(off, t), :]], axis=1)
        for h in range(2):
            s_ref[h] = _dot(ka, qa_ref[slot, h])

    def absorb(ki, s_ref, mask):
        vt = vt_ref[0, ki]
        for h in range(2):
            s = s_ref[h]
            if mask is not None:
                s = jnp.where(mask, s, -jnp.inf)
            m = m_ref[h]
            m_new = jnp.maximum(m, jnp.max(s, axis=0, keepdims=True))
            alpha = jnp.exp2(m - m_new)
            p = jnp.exp2(s - m_new)
            l_ref[h] = alpha * l_ref[h] + jnp.sum(p, axis=0, keepdims=True)
            m_ref[h] = m_new
            acc_ref[h] = alpha * acc_ref[h] + _dot(vt, p.astype(BF16))

    r_i = lax.broadcasted_iota(jnp.int32, (t, t), 0)
    c_i = lax.broadcasted_iota(jnp.int32, (t, t), 1)
    diag = r_i <= c_i

    def q_block(qi, u):
        slot = u % 2
        x_ref, y_ref = (sa_ref, sb_ref) if u in (0, 3) else (sb_ref, sa_ref)
        m_ref[...] = jnp.full(m_ref.shape, -jnp.inf, F32)
        l_ref[...] = jnp.zeros(l_ref.shape, F32)
        acc_ref[...] = jnp.zeros(acc_ref.shape, F32)

        def quad(qq, carry):
            ki = 4 * qq
            for r in range(4):
                cur, oth = (x_ref, y_ref) if r % 2 == 0 else (y_ref, x_ref)
                scores(slot, ki + r + 1, oth)
                absorb(ki + r, cur, None)
            return carry

        nquad = (qi - u) // 4
        lax.fori_loop(0, nquad, quad, 0)
        base = 4 * nquad
        cur, oth = x_ref, y_ref
        for r in range(u):
            scores(slot, base + r + 1, oth)
            absorb(base + r, cur, None)
            cur, oth = oth, cur
        nxt = jnp.minimum(qi + 1, nq - 1)
        load_q(nxt, 1 - slot)
        scores(1 - slot, 0, oth)
        absorb(qi, cur, diag)
        inv = [1.0 / l_ref[h] for h in range(2)]
        ot = jnp.where(first, acc_ref[0] * inv[0], acc_ref[1] * inv[1])
        rows = pl.ds(pl.multiple_of(qi * t, t), t)
        o_ref[rows, :] = (ot.T * _silu(fg_ref[rows, :].astype(F32))).astype(BF16)

    load_q(0, 0)
    scores(0, 0, sa_ref)
    if nq % 4 == 0:
        def group(g, carry):
            for u in range(4):
                q_block(4 * g + u, u)
            return carry

        lax.fori_loop(0, nq // 4, group, 0)
    else:
        for qi in range(nq):
            q_block(jnp.int32(qi), qi % 4)


def _attn_prompt(qt4, kb, aug, vt4, fg, *, B, T, HF, dh):
    t = qt4.shape[3]
    nq = T // t
    assert T % t == 0 and 2 * dh == LANES and HF % 2 == 0 and qt4.shape == (B, nq, HF * dh, t)
    assert nq % 4 == 0 or nq <= 8
    npair = HF // 2
    return pl.pallas_call(
        functools.partial(_attn_kernel, t=t, nq=nq, dh=dh, HF=HF),
        grid=(B, npair),
        in_specs=[pl.BlockSpec((1, nq, LANES, t), lambda b, hp: (b, 0, hp, 0)),
                  pl.BlockSpec((T, LANES), lambda b, hp: (b, hp)),
                  pl.BlockSpec((T, LANES), lambda b, hp: (b, 0)),
                  pl.BlockSpec((1, nq, LANES, t), lambda b, hp: (b, 0, hp, 0)),
                  pl.BlockSpec((T, LANES), lambda b, hp: (b, hp))],
        out_specs=pl.BlockSpec((T, LANES), lambda b, hp: (b, hp)),
        out_shape=jax.ShapeDtypeStruct((B * T, HF * dh), BF16),
        scratch_shapes=[pltpu.VMEM((2, t, t), F32), pltpu.VMEM((2, t, t), F32),
                        pltpu.VMEM((2, 1, t), F32), pltpu.VMEM((2, 1, t), F32),
                        pltpu.VMEM((2, LANES, t), F32), pltpu.VMEM((2, 2, 2 * LANES, t), BF16)],
        compiler_params=_cparams(("parallel", "parallel")),
        name="fox_prompt",
    )(qt4, kb, aug, vt4, fg)


def _sample_pre_kernel(xbc_ref, sconv_ref, fdt_ref, cw_ref, cb_ref, bias_ref, alog_ref, e_ref,
                       logf_ref, xs_ref, xdt_t_ref, ed_t_ref, bs_ref, cs_ref, nconv_ref,
                       *, H, P, G, N, HF, CW):
    DS = H * P
    cd = xbc_ref.shape[1]
    x = xbc_ref[...]
    rows = [sconv_ref[:, kk * cd:(kk + 1) * cd] for kk in range(CW - 1)] + [x]
    cw = cw_ref[...]
    conv = cb_ref[...]
    for kk in range(CW):
        conv = conv + rows[kk] * cw[kk:kk + 1, :]
    nconv_ref[...] = jnp.concatenate(rows[1:], axis=1)
    xc = _silu(conv)
    xs = xc[:, :DS]
    xs_ref[...] = xs
    bs_ref[...] = xc[:, DS:DS + G * N]
    cs_ref[...] = xc[:, DS + G * N:]

    lane = lax.broadcasted_iota(jnp.int32, (1, LANES), 1)
    is_dt = jnp.logical_and(lane >= HF, lane < HF + H)
    sp, lsg = _softplus_terms(fdt_ref[...] + bias_ref[...])
    logf_ref[...] = lsg[:, :HF]
    dt = jnp.where(is_dt, sp, 0.0)
    ed = jnp.where(is_dt, jnp.exp(dt * (-jnp.exp(alog_ref[...]))), 0.0)
    s = dt.shape[0]
    hi, mid, lo = _split3(jnp.concatenate([dt, ed], axis=0))
    ex = _dot(jnp.concatenate([hi, mid, lo], axis=0), e_ref[...])
    ex = ex[:2 * s] + ex[2 * s:4 * s] + ex[4 * s:]
    xdt_t_ref[...] = (xs * ex[:s]).T
    ed_t_ref[...] = ex[s:].T


def _sample_pre(xbc, sconv, fdt, conv_w, conv_b, bias128, alog128, e_mat, *, H, P, G, N, HF):
    s, cd = xbc.shape
    DS = H * P
    CW = conv_w.shape[0]
    kern = functools.partial(_sample_pre_kernel, H=H, P=P, G=G, N=N, HF=HF, CW=CW)
    return pl.pallas_call(
        kern,
        out_shape=[jax.ShapeDtypeStruct((s, HF), F32),
                   jax.ShapeDtypeStruct((s, DS), F32),
                   jax.ShapeDtypeStruct((DS, s), F32),
                   jax.ShapeDtypeStruct((DS, s), F32),
                   jax.ShapeDtypeStruct((s, G * N), F32),
                   jax.ShapeDtypeStruct((s, G * N), F32),
                   jax.ShapeDtypeStruct((s, (CW - 1) * cd), F32)],
        compiler_params=pltpu.CompilerParams(vmem_limit_bytes=VMEM_LIMIT),
        name="sample_pre",
    )(xbc, sconv, fdt, conv_w, conv_b, bias128, alog128, e_mat)


def _sample_state_kernel(s0_ref, xdt_t_ref, ed_t_ref, bs_ref, cs_ref, xs_ref, z_ref, dsk_ref, gssm_ref,
                         new_ref, y_ref, *, H, P, G, N):
    si = pl.program_id(0)
    DS = H * P
    DSG = DS // G
    ns = xdt_t_ref.shape[1]
    lane = lax.broadcasted_iota(jnp.int32, (1, ns), 1)
    pick = lane == si
    xcol = jnp.sum(jnp.where(pick, xdt_t_ref[...], 0.0), axis=1, keepdims=True)
    ecol = jnp.sum(jnp.where(pick, ed_t_ref[...], 0.0), axis=1, keepdims=True)
    bs = bs_ref[0]
    cs = cs_ref[0]
    ys = []
    for g in range(G):
        gs = slice(g * DSG, (g + 1) * DSG)
        new_g = s0_ref[0, gs, :] * ecol[gs] + xcol[gs] * bs[:, g * N:(g + 1) * N]
        new_ref[0, gs, :] = new_g
        chi, clo = _split2(cs[:, g * N:(g + 1) * N])
        cst = jnp.concatenate([chi, clo, jnp.zeros((SUBLANES - 2, N), BF16)], axis=0)
        nhi, nlo = _split2(new_g)
        r = _dot_nt(cst, nhi) + _dot_nt(cst, nlo)
        ys.append(r[0:1] + r[1:2])
    y = jnp.concatenate(ys, axis=1) + dsk_ref[...] * xs_ref[0]
    yg = y * _silu(z_ref[0].astype(F32))
    inv = lax.rsqrt(jnp.mean(yg * yg, axis=-1, keepdims=True) + RMS_EPS)
    y_ref[0] = ((yg * inv) * gssm_ref[...]).astype(BF16)


def _sample_state(s0, xdt_t, ed_t, bs, cs, xs, z, dsk_e, g_ssm, *, H, P, G, N):
    ns, DS, _ = s0.shape
    row3 = lambda a: a.reshape(ns, 1, a.shape[-1])
    per = lambda w: pl.BlockSpec((1, 1, w), lambda s: (s, 0, 0))
    const2 = lambda s: (0, 0)
    new, y = pl.pallas_call(
        functools.partial(_sample_state_kernel, H=H, P=P, G=G, N=N),
        grid=(ns,),
        in_specs=[pl.BlockSpec((1, DS, N), lambda s: (s, 0, 0)),
                  pl.BlockSpec((DS, ns), const2),
                  pl.BlockSpec((DS, ns), const2),
                  per(G * N), per(G * N), per(DS), per(DS),
                  pl.BlockSpec((1, DS), const2),
                  pl.BlockSpec((1, DS), const2)],
        out_specs=[pl.BlockSpec((1, DS, N), lambda s: (s, 0, 0)), per(DS)],
        out_shape=[jax.ShapeDtypeStruct((ns, DS, N), F32),
                   jax.ShapeDtypeStruct((ns, 1, DS), BF16)],
        compiler_params=_cparams(("parallel",)),
        name="sample_state",
    )(s0, xdt_t, ed_t, row3(bs), row3(cs), row3(xs), row3(z), dsk_e, g_ssm)
    return new, y.reshape(ns, DS)


_PAGES_PER_STEP = 16


def _paged_kernel(pt_ref, q_ref, kn_ref, vn_ref, lfn_ref, fg_ref, ind_ref, *rest, PP, HF, DF):
    k_refs = rest[0:PP]
    v_refs = rest[PP:2 * PP]
    lf_refs = rest[2 * PP:3 * PP]
    o_ref = rest[3 * PP]
    m_ref, l_ref, r_ref, acc_ref, qrow_ref = rest[3 * PP + 1:]
    j = pl.program_id(1)
    nj = pl.num_programs(1)
    ps = k_refs[0].shape[2]
    ind = ind_ref[...]

    @pl.when(j == 0)
    def _():
        qrow = ind * q_ref[0].astype(F32)
        qrow_ref[...] = qrow.astype(BF16)
        m_ref[...] = jnp.sum(qrow * kn_ref[0].astype(F32), axis=1, keepdims=True)
        l_ref[...] = jnp.ones_like(l_ref)
        r_ref[...] = lfn_ref[0]
        acc_ref[...] = ind * vn_ref[0].astype(F32)

    kcat = jnp.concatenate([k_refs[r][0].astype(BF16) for r in range(PP)], axis=1)
    s = _dot(qrow_ref[...], kcat)

    lfs = [lf_refs[r][0] for r in range(PP)]
    r_i = lax.broadcasted_iota(jnp.int32, (ps, ps), 0)
    c_i = lax.broadcasted_iota(jnp.int32, (ps, ps), 1)
    later = jnp.where(r_i > c_i, 1.0, 0.0).astype(BF16)
    hi, mid, lo = _split3(jnp.concatenate(lfs, axis=0))
    n = PP * HF
    suf3 = _dot(jnp.concatenate([hi, mid, lo], axis=0), later)
    suf = suf3[:n] + suf3[n:2 * n] + suf3[2 * n:]
    rcur = r_ref[...]
    biases = []
    for r in range(PP):
        biases.append(suf[r * HF:(r + 1) * HF] + rcur)
        rcur = rcur + jnp.sum(lfs[r], axis=1, keepdims=True)
    r_ref[...] = rcur
    s = s + jnp.concatenate(biases, axis=1) * LOG2E

    m_old = m_ref[...]
    m_new = jnp.maximum(m_old, jnp.max(s, axis=1, keepdims=True))
    alpha = jnp.exp2(m_old - m_new)
    p = jnp.exp2(s - m_new)
    l_ref[...] = alpha * l_ref[...] + jnp.sum(p, axis=1, keepdims=True)
    m_ref[...] = m_new
    vcat = jnp.concatenate([v_refs[r][0].astype(BF16) for r in range(PP)], axis=1)
    acc_ref[...] = alpha * acc_ref[...] + _dot_nt(p.astype(BF16), vcat)

    @pl.when(j == nj - 1)
    def _():
        o = jnp.sum(acc_ref[...] * ind / l_ref[...], axis=0, keepdims=True)
        o_ref[0] = (o * _silu(fg_ref[0].astype(F32))).astype(BF16)


def _paged_attn(page_idx, q, kn, vn, lfn, fg, ind, ckt, cvt, clft, *, HF, DF):
    ns, npages = page_idx.shape
    ps = ckt.shape[2]
    PP = min(_PAGES_PER_STEP, npages)
    assert npages % PP == 0
    nj = npages // PP
    row3 = lambda a: a.reshape(ns, 1, a.shape[-1])
    per = lambda w: pl.BlockSpec((1, 1, w), lambda s, j, pt: (s, 0, 0))

    def page_spec(r, rows):
        return pl.BlockSpec((1, rows, ps), lambda s, j, pt: (pt[s, npages - 1 - (j * PP + r)], 0, 0))

    grid_spec = pltpu.PrefetchScalarGridSpec(
        num_scalar_prefetch=1,
        grid=(ns, nj),
        in_specs=[per(DF), per(DF), per(DF),
                  pl.BlockSpec((1, HF, 1), lambda s, j, pt: (s, 0, 0)),
                  per(DF),
                  pl.BlockSpec((HF, DF), lambda s, j, pt: (0, 0))]
                 + [page_spec(r, DF) for r in range(PP)]
                 + [page_spec(r, DF) for r in range(PP)]
                 + [page_spec(r, HF) for r in range(PP)],
        out_specs=per(DF),
        scratch_shapes=[pltpu.VMEM((HF, 1), F32), pltpu.VMEM((HF, 1), F32), pltpu.VMEM((HF, 1), F32),
                        pltpu.VMEM((HF, DF), F32), pltpu.VMEM((HF, DF), BF16)],
    )
    out = pl.pallas_call(
        functools.partial(_paged_kernel, PP=PP, HF=HF, DF=DF),
        grid_spec=grid_spec,
        out_shape=jax.ShapeDtypeStruct((ns, 1, DF), BF16),
        compiler_params=_cparams(("parallel", "arbitrary")),
        name="fox_paged",
    )(page_idx, row3(q), row3(kn), row3(vn), lfn.reshape(ns, HF, 1), row3(fg), ind,
      *([ckt] * PP), *([cvt] * PP), *([clft] * PP))
    return out.reshape(ns, DF)


def _final_kernel(ys_ref, yf_ref, ga_ref, gb_ref, x_ref, pe_ref,
                  wso_ref, wfo_ref, wo_ref, wpg_ref, wple_ref, gpost_ref, gple_ref, o_ref):
    a = _dot(ys_ref[...], wso_ref[...])
    b = _dot(yf_ref[...], wfo_ref[...])
    m = jax.nn.sigmoid(ga_ref[...].astype(F32)) * a + jax.nn.sigmoid(gb_ref[...].astype(F32)) * b
    o = _dot(m.astype(BF16), wo_ref[...])
    inv = lax.rsqrt(jnp.mean(o * o, axis=-1, keepdims=True) + RMS_EPS)
    x1 = x_ref[...] + (o * inv) * gpost_ref[...]
    inv1 = lax.rsqrt(jnp.mean(x1 * x1, axis=-1, keepdims=True) + RMS_EPS)
    hg = ((x1 * inv1) * gple_ref[...]).astype(BF16)
    gate = jax.nn.sigmoid(_dot(hg, wpg_ref[...]))
    pe = _dot(pe_ref[...].astype(BF16), wple_ref[...])
    o_ref[...] = x1 + gate * pe


def _final(ys, yf, ga, gb, x2, pemb, wso, wfo, wo, wpg, wple, g_post, g_ple):
    m, d = x2.shape
    tm = min(512, m)
    assert m % tm == 0
    tok = lambda w: pl.BlockSpec((tm, w), lambda i: (i, 0))
    full = lambda a: pl.BlockSpec(a.shape, lambda i: (0, 0))
    gp = g_post.reshape(1, d)
    gl = g_ple.reshape(1, d)
    return pl.pallas_call(
        _final_kernel,
        grid=(m // tm,),
        in_specs=[tok(ys.shape[1]), tok(yf.shape[1]), tok(d), tok(d), tok(d), tok(pemb.shape[1]),
                  full(wso), full(wfo), full(wo), full(wpg), full(wple), full(gp), full(gl)],
        out_specs=tok(d),
        out_shape=jax.ShapeDtypeStruct((m, d), F32),
        compiler_params=_cparams(("parallel",)),
        name="merge_out",
    )(ys, yf, ga, gb, x2, pemb, wso, wfo, wo, wpg, wple, gp, gl)


def kernel(x_prompt, x_sample, cache_k, cache_v, cache_logf, state_ssm, state_conv, page_table, p_prompt, p_sample, g_pre, w_in, conv_w, conv_b, dt_bias, a_log, d_skip, g_ssm, b_fgate, w_ssm_out, w_fox_out, w_out, g_post, w_ple, w_ple_gate, g_ple):
    B, T, D = x_prompt.shape
    S, TS, _ = x_sample.shape
    assert TS == 1
    depth = w_in.shape[0]
    _, n_phys, ps, HF, dh = cache_k.shape
    _, _, H, P, N = state_ssm.shape
    CW, cd = conv_w.shape[1], conv_w.shape[2]
    DS = H * P
    DF = HF * dh
    G = (cd - DS) // (2 * N)
    L = LANES
    q_scale = float(dh) ** -0.5 * LOG2E
    pad_l = LANES - HF - H

    ch_s = jnp.arange(DS) // P
    e_ssm = (jnp.arange(LANES)[:, None] == (HF + ch_s)[None, :]).astype(BF16)
    ind_fox = (jnp.arange(HF)[:, None] == (jnp.arange(DF) // dh)[None, :]).astype(F32)
    src = jnp.arange(3 * LANES)
    dst = (src // LANES) * HF + (src % LANES)
    perm = jnp.logical_and((src % LANES < HF)[:, None], dst[:, None] == jnp.arange(LANES)[None, :]).astype(BF16)

    ckt = jnp.transpose(cache_k, (0, 1, 3, 4, 2)).reshape(depth * n_phys, DF, ps)
    cvt = jnp.transpose(cache_v, (0, 1, 3, 4, 2)).reshape(depth * n_phys, DF, ps)
    clft = jnp.transpose(cache_logf, (0, 1, 3, 2)).reshape(depth * n_phys, HF, ps)

    xp = x_prompt.reshape(B * T, D)
    xs_tok = x_sample.reshape(S, D)
    outs_p = ([], [], [], [], [])
    outs_s = ([], [], [], [], [])
    for i in range(depth):
        z_w, xbc_w, dt_w, q_w, k_w, v_w, f_w, fg_w, ga_w, gb_w = jnp.split(
            w_in[i], [DS, DS + cd, DS + cd + H, DS + cd + H + DF, DS + cd + H + 2 * DF,
                      DS + cd + H + 3 * DF, DS + cd + H + 3 * DF + HF, DS + cd + H + 4 * DF + HF,
                      DS + cd + H + 4 * DF + HF + D], axis=1)
        w_tok = jnp.concatenate([z_w, fg_w, ga_w, gb_w, xbc_w], axis=1).astype(BF16)
        tok_widths = (DS, DF, D, D, cd)
        w_qkv = jnp.concatenate([q_w, k_w, v_w], axis=1).astype(BF16)
        w_small = jnp.concatenate([f_w, dt_w, jnp.zeros((D, pad_l), F32)], axis=1).astype(BF16)
        bias128 = jnp.concatenate([b_fgate[i], dt_bias[i], jnp.zeros((pad_l,), F32)]).reshape(1, LANES)
        alog128 = jnp.concatenate([jnp.zeros((HF,), F32), a_log[i], jnp.zeros((pad_l,), F32)]).reshape(1, LANES)
        dsk_e = jnp.repeat(d_skip[i], P).reshape(1, DS)
        gssm = g_ssm[i].reshape(1, DS)
        cw_i = conv_w[i]
        cb_i = conv_b[i].reshape(1, cd)
        wso = w_ssm_out[i].astype(BF16)
        wfo = w_fox_out[i].astype(BF16)
        wo = w_out[i].astype(BF16)
        wpg = w_ple_gate[i].astype(BF16)
        wple = w_ple[i].astype(BF16)
        proj_tok = functools.partial(_inproj_tok, g_pre=g_pre[i], w_tok=w_tok, w_small=w_small,
                                     widths=tok_widths, dtypes=(BF16, BF16, BF16, BF16, F32))
        proj_qkv = functools.partial(_inproj_qkv, g_pre=g_pre[i], w_qkv=w_qkv, d_fox=DF, q_scale=q_scale)
        fin = functools.partial(_final, wso=wso, wfo=wfo, wo=wo, wpg=wpg, wple=wple,
                                g_post=g_post[i], g_ple=g_ple[i])

        z, fg, ga, gb, xbc, fdt = proj_tok(xp)
        _, qt4, kb, kt, _, vt, vt4 = proj_qkv(xp, B=B, T=T)
        hist8 = jnp.zeros((B, SUBLANES, cd), F32)
        init = jnp.zeros((B, DS, N), F32)
        y_ssm, logft, aug, ssm_fin = _ssd_prompt(xbc, z, fdt, hist8, init, cw_i, cb_i, bias128, alog128,
                                                 dsk_e, gssm, e_ssm, perm, B=B, T=T, L=L, H=H, P=P, G=G, N=N, HF=HF)
        y_fox = _attn_prompt(qt4, kb, aug, vt4, fg, B=B, T=T, HF=HF, dh=dh)
        xp = fin(y_ssm, y_fox, ga, gb, xp, p_prompt[i].reshape(B * T, -1))
        xbc3 = xbc.reshape(B, T, cd)
        hist_p = jnp.zeros((B, CW - 1, cd), F32)
        conv_p = jnp.concatenate([hist_p, xbc3], axis=1)[:, T:] if T < CW - 1 else xbc3[:, T - (CW - 1):]
        k_out = jnp.transpose(kt.reshape(B, HF, dh, T), (0, 3, 1, 2))
        v_out = jnp.transpose(vt.reshape(B, HF, dh, T), (0, 3, 1, 2))
        for lst, arr in zip(outs_p, (k_out, v_out, jnp.transpose(logft, (0, 2, 1)),
                                     ssm_fin.reshape(B, H, P, N), conv_p)):
            lst.append(arr)

        z, fg, ga, gb, xbc, fdt = proj_tok(xs_tok)
        qb, _, kb, kt, vb, vt, _ = proj_qkv(xs_tok, B=1, T=S)
        logf_s, xs_c, xdt_t, ed_t, bs, cs, nconv = _sample_pre(
            xbc, state_conv[i].reshape(S, (CW - 1) * cd), fdt, cw_i, cb_i, bias128, alog128, e_ssm,
            H=H, P=P, G=G, N=N, HF=HF)
        ssm_new, y_ssm = _sample_state(state_ssm[i].reshape(S, DS, N), xdt_t, ed_t, bs, cs, xs_c, z,
                                       dsk_e, gssm, H=H, P=P, G=G, N=N)
        y_fox = _paged_attn(page_table + i * n_phys, qb, kb, vb, logf_s, fg, ind_fox, ckt, cvt, clft, HF=HF, DF=DF)
        xs_tok = fin(y_ssm, y_fox, ga, gb, xs_tok, p_sample[i].reshape(S, -1))
        k_out = jnp.transpose(kt.reshape(HF, dh, S), (2, 0, 1)).reshape(S, 1, HF, dh)
        v_out = jnp.transpose(vt.reshape(HF, dh, S), (2, 0, 1)).reshape(S, 1, HF, dh)
        for lst, arr in zip(outs_s, (k_out, v_out, logf_s.reshape(S, 1, HF),
                                     ssm_new.reshape(S, H, P, N), nconv.reshape(S, CW - 1, cd))):
            lst.append(arr)

    k_p, v_p, lf_p, ssm_p, conv_p = [jnp.stack(a) for a in outs_p]
    k_s, v_s, lf_s, ssm_s, conv_s = [jnp.stack(a) for a in outs_s]
    return (xp.reshape(B, T, D), xs_tok.reshape(S, 1, D), k_p, v_p, lf_p, ssm_p, conv_p,
            k_s, v_s, lf_s, ssm_s, conv_s)
```
